```python
import jax, jax.numpy as jnp
from jax import lax
import numpy as np

D_MODEL = 1024
BATCH = 16
SEQ = 4096
DEPTH = 1
DEC_BATCH = 16
DEC_SEQ = 32
PAST_LEN = 4096

CHUNK = 64
N_HEADS_SB = 8
HEAD_DIM = 64
D_A = N_HEADS_SB * HEAD_DIM
D_C = 512
D_MIX = D_A + D_C
D_IN = 3 * D_A + 2 * D_C
CONV_W = 31
CONV_HIST = CONV_W - 1
D_FF = 2816
Q_BLOCK = 128
EPS = 1e-6

kernel_name = "stickbreak_conformer_hybrid_stream_step"


def _rmsnorm(x, g):
    xf = x.astype(jnp.float32)
    y = xf * lax.rsqrt(jnp.mean(xf * xf, axis=-1, keepdims=True) + EPS)
    return (y * g.astype(jnp.float32)).astype(x.dtype)


def _layernorm(x, g, b):
    xf = x.astype(jnp.float32)
    mu = jnp.mean(xf, axis=-1, keepdims=True)
    xc = xf - mu
    y = xc * lax.rsqrt(jnp.mean(xc * xc, axis=-1, keepdims=True) + EPS)
    return (y * g.astype(jnp.float32) + b.astype(jnp.float32)).astype(x.dtype)


def _sb_attention(q, k, v, q_pos, k_pos):
    B, H, Tq, hd = q.shape
    qb = min(Q_BLOCK, Tq)
    nb = Tq // qb
    scale = HEAD_DIM ** -0.5

    def block(args):
        q_blk, pos_blk = args
        z = jnp.einsum('bhqd,bhkd->bhqk', q_blk, k,
                       preferred_element_type=jnp.float32) * scale
        mask = k_pos[None, :] < pos_blk[:, None]
        log_1mb = jnp.where(mask, jax.nn.log_sigmoid(-z), 0.0)
        stick = lax.cumsum(log_1mb, axis=3, reverse=True) - log_1mb
        w = jnp.where(mask, jnp.exp(jax.nn.log_sigmoid(z) + stick), 0.0)
        return jnp.einsum('bhqk,bhkd->bhqd', w.astype(v.dtype), v)

    q_blocks = q.reshape(B, H, nb, qb, hd).transpose(2, 0, 1, 3, 4)
    pos_blocks = q_pos.reshape(nb, qb)
    out = lax.map(block, (q_blocks, pos_blocks))
    return out.transpose(1, 2, 0, 3, 4).reshape(B, H, Tq, hd)


def _causal_dwconv(u_ext, w, b):
    out = lax.conv_general_dilated(
        u_ext, w[:, None, :].astype(u_ext.dtype), window_strides=(1,), padding='VALID',
        dimension_numbers=('NWC', 'WIO', 'NWC'), feature_group_count=D_C)
    return out + b


def _layer(x, k_hist, v_hist, conv_hist, pos0,
           w_in, sb_norm_g, conv_w, conv_b, conv_ln_g, conv_ln_b, w_out,
           norm1_g, norm2_g, w_gate, w_up, w_down):
    B, T, _ = x.shape
    h = _rmsnorm(x, norm1_g)
    proj = h @ w_in
    q, k, v, a, g = jnp.split(proj, [D_A, 2 * D_A, 3 * D_A, 3 * D_A + D_C], axis=-1)

    def heads(t):
        return t.reshape(B, T, N_HEADS_SB, HEAD_DIM).transpose(0, 2, 1, 3)

    q, k, v = heads(q), heads(k), heads(v)
    k_all = k if k_hist is None else jnp.concatenate([k_hist, k], axis=2)
    v_all = v if v_hist is None else jnp.concatenate([v_hist, v], axis=2)
    q_pos = pos0 + jnp.arange(T, dtype=jnp.int32)
    k_pos = jnp.arange(k_all.shape[2], dtype=jnp.int32)
    o = _sb_attention(q, k_all, v_all, q_pos, k_pos)
    o = _rmsnorm(o, sb_norm_g[:, None, :])
    o = o.transpose(0, 2, 1, 3).reshape(B, T, D_A)

    u = a * jax.nn.sigmoid(g)
    u_ext = jnp.concatenate([conv_hist, u], axis=1)
    c = _causal_dwconv(u_ext, conv_w, conv_b)
    c = jax.nn.silu(_layernorm(c, conv_ln_g, conv_ln_b))

    x = x + jnp.concatenate([o, c], axis=-1) @ w_out
    hf = _rmsnorm(x, norm2_g)
    x = x + (jax.nn.silu(hf @ w_gate) * (hf @ w_up)) @ w_down
    return x, k, v, u_ext[:, -CONV_HIST:]


def setup_inputs(seed: int = 0) -> dict:
    key = jax.random.key(seed)
    ks = jax.random.split(key, 20)
    f32 = jnp.float32
    nrm = lambda k, shape, s: jax.random.normal(k, shape, f32) * s
    return {
        "x_prompt": nrm(ks[0], (BATCH, SEQ, D_MODEL), 1.0),
        "x_sample": nrm(ks[1], (DEC_BATCH, DEC_SEQ, D_MODEL), 1.0),
        "cache_k": nrm(ks[2], (DEPTH, DEC_BATCH, N_HEADS_SB, PAST_LEN, HEAD_DIM), 1.0),
        "cache_v": nrm(ks[3], (DEPTH, DEC_BATCH, N_HEADS_SB, PAST_LEN, HEAD_DIM), 1.0),
        "state_conv": nrm(ks[4], (DEPTH, DEC_BATCH, CONV_HIST, D_C), 0.5),
        "w_in": nrm(ks[5], (DEPTH, D_MODEL, D_IN), D_MODEL ** -0.5),
        "sb_norm_g": 1.0 + nrm(ks[6], (DEPTH, N_HEADS_SB, HEAD_DIM), 0.02),
        "conv_w": nrm(ks[7], (DEPTH, CONV_W, D_C), CONV_W ** -0.5),
        "conv_b": nrm(ks[8], (DEPTH, D_C), 0.02),
        "conv_ln_g": 1.0 + nrm(ks[9], (DEPTH, D_C), 0.02),
        "conv_ln_b": nrm(ks[10], (DEPTH, D_C), 0.02),
        "w_out": nrm(ks[11], (DEPTH, D_MIX, D_MODEL), D_MIX ** -0.5),
        "norm1_g": 1.0 + nrm(ks[12], (DEPTH, D_MODEL), 0.02),
        "norm2_g": 1.0 + nrm(ks[13], (DEPTH, D_MODEL), 0.02),
        "w_gate": nrm(ks[14], (DEPTH, D_MODEL, D_FF), D_MODEL ** -0.5),
        "w_up": nrm(ks[15], (DEPTH, D_MODEL, D_FF), D_MODEL ** -0.5),
        "w_down": nrm(ks[16], (DEPTH, D_FF, D_MODEL), D_FF ** -0.5),
        "final_g": 1.0 + nrm(ks[17], (D_MODEL,), 0.02),
    }


def reference(x_prompt, x_sample, cache_k, cache_v, state_conv,
              w_in, sb_norm_g, conv_w, conv_b, conv_ln_g, conv_ln_b, w_out,
              norm1_g, norm2_g, w_gate, w_up, w_down, final_g):
    hp, hs = x_prompt, x_sample
    kp_l, vp_l, cp_l, ks_l, vs_l, cs_l = [], [], [], [], [], []
    past = cache_k.shape[3]
    for l in range(DEPTH):
        params = (w_in[l], sb_norm_g[l], conv_w[l], conv_b[l], conv_ln_g[l], conv_ln_b[l],
                  w_out[l], norm1_g[l], norm2_g[l], w_gate[l], w_up[l], w_down[l])
        zero_hist = jnp.zeros((hp.shape[0], CONV_HIST, D_C), hp.dtype)
        hp, kp, vp, cp = _layer(hp, None, None, zero_hist, 0, *params)
        hs, ksm, vsm, csm = _layer(hs, cache_k[l], cache_v[l], state_conv[l], past, *params)
        kp_l.append(kp); vp_l.append(vp); cp_l.append(cp)
        ks_l.append(ksm); vs_l.append(vsm); cs_l.append(csm)
    y_prompt = _rmsnorm(hp, final_g)
    y_sample = _rmsnorm(hs, final_g)
    k_prompt = jnp.stack(kp_l)
    v_prompt = jnp.stack(vp_l)
    conv_prompt = jnp.stack(cp_l)
    k_sample = jnp.stack(ks_l)
    v_sample = jnp.stack(vs_l)
    conv_sample = jnp.stack(cs_l)
    return (y_prompt, y_sample, k_prompt, v_prompt, conv_prompt, k_sample, v_sample, conv_sample)
```

```python
import functools
import math

import jax
import jax.numpy as jnp
from jax import lax
from jax.experimental import pallas as pl
from jax.experimental.pallas import tpu as pltpu

N_HEADS = 8
HEAD_DIM = 64
D_A = N_HEADS * HEAD_DIM
D_C = 512
CONV_W = 31
CONV_HIST = CONV_W - 1
EPS = 1e-6

LANES = 128
HALO = 32
KEY_BLOCK = 256
FF_CHUNK = 256
ROW_TILE = 512
VMEM_LIMIT = 56 * 1024 * 1024

LOG2E = 1.4426950408889634
Q_SCALE = HEAD_DIM ** -0.5 * LOG2E
MASKED = -1e30

BF16 = jnp.bfloat16
F32 = jnp.float32


def _rms_scale(x):
    return lax.rsqrt(jnp.mean(x * x, axis=-1, keepdims=True) + EPS)


def _sigmoid(x):
    return 0.5 + 0.5 * jnp.tanh(0.5 * x)


def _in_proj_kernel(x_ref, g1_ref, w_ref, q_ref, kb_ref, vb_ref, kf_ref, vf_ref, u_ref):
    nb, tm, d = x_ref.shape
    m = nb * tm
    x = x_ref[...].reshape(m, d)
    h = (x * _rms_scale(x) * g1_ref[...]).astype(BF16)

    def proj(c):
        return jnp.dot(h, w_ref[c], preferred_element_type=F32)

    q_ref[...] = (proj(0) * Q_SCALE).astype(BF16).reshape(nb, tm, D_A)
    for c, b_ref, f_ref in ((1, kb_ref, kf_ref), (2, vb_ref, vf_ref)):
        p = proj(c)
        b_ref[...] = p.astype(BF16).reshape(nb, tm, D_A)
        for hd in range(N_HEADS):
            f_ref[0, :, hd, :, :] = p[:, hd * HEAD_DIM:(hd + 1) * HEAD_DIM].reshape(nb, tm, HEAD_DIM)
    u_ref[...] = (proj(3) * _sigmoid(proj(4))).reshape(nb, tm, D_C)


def _in_proj(x, g1, w5, nb, tm):
    bsz, t, d = x.shape
    grid = (bsz // nb, t // tm)
    row = lambda b, i: (b, i, 0)
    const2 = lambda b, i: (0, 0)
    const3 = lambda b, i: (0, 0, 0)
    act = lambda width: pl.BlockSpec((nb, tm, width), row)
    kv = pl.BlockSpec((1, nb, N_HEADS, tm, HEAD_DIM), lambda b, i: (0, b, 0, i, 0))
    return pl.pallas_call(
        _in_proj_kernel,
        grid=grid,
        in_specs=[act(d),
                  pl.BlockSpec((1, d), const2),
                  pl.BlockSpec(w5.shape, const3, pipeline_mode=pl.Buffered(1))],
        out_specs=[act(D_A), act(D_A), act(D_A), kv, kv, act(D_C)],
        out_shape=[jax.ShapeDtypeStruct((bsz, t, D_A), BF16)] * 3
        + [jax.ShapeDtypeStruct((1, bsz, N_HEADS, t, HEAD_DIM), F32)] * 2
        + [jax.ShapeDtypeStruct((bsz, t, D_C), F32)],
        compiler_params=pltpu.CompilerParams(
            dimension_semantics=("arbitrary", "arbitrary"), vmem_limit_bytes=VMEM_LIMIT),
        name="in_proj",
    )(x, g1, w5)


def _sb_block(q, k, v, carry, tri, mask):
    z = lax.dot_general(q, k, (((1,), (1,)), ((), ())), preferred_element_type=F32)
    if mask is not None:
        z = jnp.where(mask, z, MASKED)
    sp = jnp.maximum(z, 0.0) + jnp.log(1.0 + jnp.exp2(-jnp.abs(z))) * LOG2E
    c = jnp.dot(sp.astype(BF16), tri, preferred_element_type=F32) + carry
    w = jnp.exp2(z - c)
    pv = jnp.dot(w.astype(BF16), v, preferred_element_type=F32)
    return pv, c[:, :1]


def _head_norm(o, g):
    lane = lax.broadcasted_iota(jnp.int32, o.shape, 1)
    first = lane < HEAD_DIM
    sq = o * o
    ms_a = jnp.sum(jnp.where(first, sq, 0.0), axis=-1, keepdims=True)
    ms_b = jnp.sum(jnp.where(first, 0.0, sq), axis=-1, keepdims=True)
    ms = jnp.where(first, ms_a, ms_b) * (1.0 / HEAD_DIM)
    return o * lax.rsqrt(ms + EPS) * g


def _sb_prompt_kernel(q_ref, k_ref, v_ref, tri_ref, g_ref, o_ref, acc_ref, carry_ref):
    i = pl.program_id(2)
    tq = q_ref.shape[1]
    q2 = q_ref[0]
    tri = tri_ref[...]
    lane = lax.broadcasted_iota(jnp.int32, q2.shape, 1)
    zero = jnp.zeros_like(q2)
    qs = (jnp.where(lane < HEAD_DIM, q2, zero), jnp.where(lane < HEAD_DIM, zero, q2))

    row = lax.broadcasted_iota(jnp.int32, (tq, KEY_BLOCK), 0)
    col = lax.broadcasted_iota(jnp.int32, (tq, KEY_BLOCK), 1)
    diag = col < row

    def visit(kb, mask):
        start = pl.multiple_of(kb * KEY_BLOCK, KEY_BLOCK)
        k = k_ref[0, pl.ds(start, KEY_BLOCK), :]
        v = v_ref[0, pl.ds(start, KEY_BLOCK), :]
        for hh in range(2):
            pv, carry = _sb_block(qs[hh], k, v, carry_ref[hh], tri, mask)
            acc_ref[hh] += pv
            carry_ref[hh] = carry

    acc_ref[...] = jnp.zeros_like(acc_ref)
    carry_ref[...] = jnp.zeros_like(carry_ref)
    visit(i, diag)

    def body(j, _):
        visit(i - 1 - j, None)
        return 0

    lax.fori_loop(0, i, body, 0)

    lane_o = lax.broadcasted_iota(jnp.int32, (tq, LANES), 1)
    o = jnp.where(lane_o < HEAD_DIM, acc_ref[0], acc_ref[1])
    o_ref[0] = _head_norm(o, g_ref[...]).astype(BF16)


def _sb_prompt(q, k, v, tri, g):
    bsz, t, _ = q.shape
    tq = KEY_BLOCK
    pairs = D_A // LANES
    return pl.pallas_call(
        _sb_prompt_kernel,
        grid=(bsz, pairs, t // tq),
        in_specs=[pl.BlockSpec((1, tq, LANES), lambda b, p, i: (b, i, p)),
                  pl.BlockSpec((1, t, LANES), lambda b, p, i: (b, 0, p)),
                  pl.BlockSpec((1, t, LANES), lambda b, p, i: (b, 0, p)),
                  pl.BlockSpec(tri.shape, lambda b, p, i: (0, 0)),
                  pl.BlockSpec((1, LANES), lambda b, p, i: (0, p))],
        out_specs=pl.BlockSpec((1, tq, LANES), lambda b, p, i: (b, i, p)),
        out_shape=jax.ShapeDtypeStruct((bsz, t, D_A), BF16),
        scratch_shapes=[pltpu.VMEM((2, tq, LANES), F32), pltpu.VMEM((2, tq, 1), F32)],
        compiler_params=pltpu.CompilerParams(
            dimension_semantics=("arbitrary", "arbitrary", "arbitrary"),
            vmem_limit_bytes=VMEM_LIMIT),
        name="sb_prompt",
    )(q, k, v, tri, g)


def _sb_sample_kernel(q_ref, kn_ref, vn_ref, ck_ref, cv_ref, tri_ref, g_ref, o_ref):
    ts = q_ref.shape[1]
    past = ck_ref.shape[3]
    tri = tri_ref[...]
    row = lax.broadcasted_iota(jnp.int32, (ts, KEY_BLOCK), 0)
    col = lax.broadcasted_iota(jnp.int32, (ts, KEY_BLOCK), 1)
    diag = col < row
    pad = jnp.zeros((KEY_BLOCK - ts, HEAD_DIM), BF16)

    outs = []
    for hh in range(2):
        sl = slice(hh * HEAD_DIM, (hh + 1) * HEAD_DIM)
        q = q_ref[0][:, sl]
        k_new = jnp.concatenate([kn_ref[0][:, sl], pad], axis=0)
        v_new = jnp.concatenate([vn_ref[0][:, sl], pad], axis=0)
        acc, carry = _sb_block(q, k_new, v_new, jnp.zeros((ts, 1), F32), tri, diag)

        def body(j, state, hh=hh, q=q):
            acc, carry = state
            start = pl.multiple_of(past - (j + 1) * KEY_BLOCK, KEY_BLOCK)
            k = ck_ref[0, 0, hh, pl.ds(start, KEY_BLOCK), :].astype(BF16)
            v = cv_ref[0, 0, hh, pl.ds(start, KEY_BLOCK), :].astype(BF16)
            pv, carry = _sb_block(q, k, v, carry, tri, None)
            return acc + pv, carry

        acc, _ = lax.fori_loop(0, past // KEY_BLOCK, body, (acc, carry))
        outs.append(acc)
    o = jnp.concatenate(outs, axis=-1)
    o_ref[0] = _head_norm(o, g_ref[...]).astype(BF16)


def _sb_sample(q, kn, vn, cache_k, cache_v, tri, g):
    bsz, ts, _ = q.shape
    past = cache_k.shape[3]
    pairs = D_A // LANES
    new = pl.BlockSpec((1, ts, LANES), lambda b, p: (b, 0, p))
    cache = pl.BlockSpec((1, 1, 2, past, HEAD_DIM), lambda b, p: (0, b, p, 0, 0))
    return pl.pallas_call(
        _sb_sample_kernel,
        grid=(bsz, pairs),
        in_specs=[new, new, new, cache, cache,
                  pl.BlockSpec(tri.shape, lambda b, p: (0, 0)),
                  pl.BlockSpec((1, LANES), lambda b, p: (0, p))],
        out_specs=new,
        out_shape=jax.ShapeDtypeStruct((bsz, ts, D_A), BF16),
        compiler_params=pltpu.CompilerParams(
            dimension_semantics=("arbitrary", "arbitrary"), vmem_limit_bytes=VMEM_LIMIT),
        name="sb_sample",
    )(q, kn, vn, cache_k, cache_v, tri, g)


def _out_ffn_kernel(x_ref, o_ref, u_ref, halo_ref, cw_ref, cb_ref, lg_ref, lb_ref,
                    wo_ref, g2_ref, wg_ref, wu_ref, wd_ref, gf_ref, y_ref,
                    ext_ref, acc_ref, *, zero_first_halo):
    nb, tm, d = x_ref.shape
    m = nb * tm

    halo = halo_ref[...]
    if zero_first_halo:
        halo = jnp.where(pl.program_id(1) == 0, 0.0, halo)
    ext_ref[:, :HALO, :] = halo
    ext_ref[:, HALO:, :] = u_ref[...]
    conv = jnp.zeros((nb, tm, D_C), F32) + cb_ref[...]
    for tap in range(CONV_W):
        off = HALO - CONV_HIST + tap
        conv = conv + ext_ref[:, off:off + tm, :] * cw_ref[tap:tap + 1, :]
    conv = conv.reshape(m, D_C)
    mu = jnp.mean(conv, axis=-1, keepdims=True)
    cc = conv - mu
    ln = cc * lax.rsqrt(jnp.mean(cc * cc, axis=-1, keepdims=True) + EPS) * lg_ref[...] + lb_ref[...]
    c_act = (ln * _sigmoid(ln)).astype(BF16)

    mix = jnp.dot(o_ref[...].reshape(m, D_A), wo_ref[0], preferred_element_type=F32)
    mix = mix + jnp.dot(c_act, wo_ref[1], preferred_element_type=F32)
    x1 = x_ref[...].reshape(m, d) + mix

    hf = (x1 * _rms_scale(x1) * g2_ref[...]).astype(BF16)
    acc_ref[...] = x1

    def ffn(c, _):
        gate = jnp.dot(hf, wg_ref[c], preferred_element_type=F32)
        up = jnp.dot(hf, wu_ref[c], preferred_element_type=F32)
        act = (gate * _sigmoid(gate) * up).astype(BF16)
        acc_ref[...] += jnp.dot(act, wd_ref[c], preferred_element_type=F32)
        return 0

    lax.fori_loop(0, wg_ref.shape[0], ffn, 0)
    x2 = acc_ref[...]
    y_ref[...] = (x2 * _rms_scale(x2) * gf_ref[...]).reshape(nb, tm, d)


def _out_ffn(x, o, u, halo_src, zero_first_halo, cw, cb, lg, lb, wo2, g2, wg, wu, wd, gf, nb, tm):
    bsz, t, d = x.shape
    grid = (bsz // nb, t // tm)
    row = lambda b, i: (b, i, 0)
    const2 = lambda b, i: (0, 0)
    const3 = lambda b, i: (0, 0, 0)
    if zero_first_halo:
        per_tile = tm // HALO
        halo_map = lambda b, i: (b, jnp.maximum(i * per_tile - 1, 0), 0)
    else:
        halo_map = lambda b, i: (b, 0, 0)
    vec = lambda a: pl.BlockSpec(a.shape, const2)
    resident = lambda a: pl.BlockSpec(a.shape, const3, pipeline_mode=pl.Buffered(1))
    return pl.pallas_call(
        functools.partial(_out_ffn_kernel, zero_first_halo=zero_first_halo),
        grid=grid,
        in_specs=[pl.BlockSpec((nb, tm, d), row),
                  pl.BlockSpec((nb, tm, D_A), row),
                  pl.BlockSpec((nb, tm, D_C), row),
                  pl.BlockSpec((nb, HALO, D_C), halo_map),
                  vec(cw), vec(cb), vec(lg), vec(lb),
                  resident(wo2), vec(g2), resident(wg), resident(wu), resident(wd), vec(gf)],
        out_specs=pl.BlockSpec((nb, tm, d), row),
        out_shape=jax.ShapeDtypeStruct((bsz, t, d), F32),
        scratch_shapes=[pltpu.VMEM((nb, HALO + tm, D_C), F32), pltpu.VMEM((nb * tm, d), F32)],
        compiler_params=pltpu.CompilerParams(
            dimension_semantics=("arbitrary", "arbitrary"), vmem_limit_bytes=VMEM_LIMIT),
        name="out_ffn",
    )(x, o, u, halo_src, cw, cb, lg, lb, wo2, g2, wg, wu, wd, gf)


def _row(a):
    return a.reshape(1, -1)


def kernel(x_prompt, x_sample, cache_k, cache_v, state_conv, w_in, sb_norm_g, conv_w, conv_b,
           conv_ln_g, conv_ln_b, w_out, norm1_g, norm2_g, w_gate, w_up, w_down, final_g):
    assert w_in.shape[0] == 1, "single-layer step"
    d = x_prompt.shape[-1]
    d_ff = w_gate.shape[-1]
    n_ff = d_ff // FF_CHUNK

    w5 = w_in[0].astype(BF16).reshape(d, 5, D_A).transpose(1, 0, 2)
    wo2 = w_out[0].astype(BF16).reshape(2, D_A, d)
    wg = w_gate[0].astype(BF16).reshape(d, n_ff, FF_CHUNK).transpose(1, 0, 2)
    wu = w_up[0].astype(BF16).reshape(d, n_ff, FF_CHUNK).transpose(1, 0, 2)
    wd = w_down[0].astype(BF16).reshape(n_ff, FF_CHUNK, d)
    g1, g2, gf = _row(norm1_g[0]), _row(norm2_g[0]), _row(final_g)
    sbg = _row(sb_norm_g[0])
    cw, cb = conv_w[0], _row(conv_b[0])
    lg, lb = _row(conv_ln_g[0]), _row(conv_ln_b[0])
    idx = jnp.arange(KEY_BLOCK)
    tri = (idx[:, None] >= idx[None, :]).astype(BF16)

    tail = (cw, cb, lg, lb, wo2, g2, wg, wu, wd, gf)

    q, kb, vb, k_prompt, v_prompt, u = _in_proj(x_prompt, g1, w5, 1, ROW_TILE)
    o = _sb_prompt(q, kb, vb, tri, sbg)
    y_prompt = _out_ffn(x_prompt, o, u, u, True, *tail, 1, ROW_TILE)
    conv_prompt = u[None, :, -CONV_HIST:, :]

    bs, ts, _ = x_sample.shape
    qs, kbs, vbs, k_sample, v_sample, us = _in_proj(x_sample, g1, w5, bs, ts)
    os_ = _sb_sample(qs, kbs, vbs, cache_k, cache_v, tri, sbg)
    hist = jnp.pad(state_conv[0], ((0, 0), (HALO - CONV_HIST, 0), (0, 0)))
    y_sample = _out_ffn(x_sample, os_, us, hist, False, *tail, bs, ts)
    conv_sample = jnp.concatenate([state_conv[0], us], axis=1)[None, :, -CONV_HIST:, :]

    return (y_prompt, y_sample, k_prompt, v_prompt, conv_prompt, k_sample, v_sample, conv_sample)
```

```python
import functools

import jax
import jax.numpy as jnp
from jax import lax
from jax.experimental import pallas as pl
from jax.experimental.pallas import tpu as pltpu

N_HEADS = 8
HEAD_DIM = 64
D_A = N_HEADS * HEAD_DIM
D_C = 512
CONV_W = 31
CONV_HIST = CONV_W - 1
EPS = 1e-6

SUBLANES = 8
LANES = 128
PAIRS = D_A // LANES
HALO = 32
KEY_BLOCK = 256
FF_CHUNK = 256
ROW_TILE = 512
CONV_PIECE = 48
VMEM_LIMIT = 56 * 1024 * 1024

LOG2E = 1.4426950408889634
Q_SCALE = HEAD_DIM ** -0.5 * LOG2E
MASKED = -1e30
UNDERFLOW_BITS = 150.0

BF16 = jnp.bfloat16
F32 = jnp.float32

NT = (((1,), (1,)), ((), ()))


def _rms_scale(x):
    return lax.rsqrt(jnp.mean(x * x, axis=-1, keepdims=True) + EPS)


def _sigmoid(x):
    return 0.5 + 0.5 * jnp.tanh(0.5 * x)


def _in_proj_prompt_kernel(x_ref, g1_ref, wqag_ref, wkvt_ref,
                           q_ref, kb_ref, vb_ref, kf_ref, vf_ref, u_ref):
    tm = x_ref.shape[1]
    x = x_ref[0]
    h = (x * _rms_scale(x) * g1_ref[...]).astype(BF16)
    q_ref[0] = (jnp.dot(h, wqag_ref[0], preferred_element_type=F32) * Q_SCALE).astype(BF16)
    for c, b_ref, f_ref in ((0, kb_ref, kf_ref), (1, vb_ref, vf_ref)):
        pt = lax.dot_general(wkvt_ref[c], h, NT, preferred_element_type=F32)
        f_ref[0, 0] = pt.reshape(N_HEADS, HEAD_DIM, tm)
        for j in range(tm // KEY_BLOCK):
            b_ref[0, j] = pt[:, j * KEY_BLOCK:(j + 1) * KEY_BLOCK].astype(BF16)
    a = jnp.dot(h, wqag_ref[1], preferred_element_type=F32)
    g = jnp.dot(h, wqag_ref[2], preferred_element_type=F32)
    u_ref[0] = a * _sigmoid(g)


def _in_proj_prompt(x, g1, wqag, wkvt):
    bsz, t, d = x.shape
    tm = ROW_TILE
    per_tile = tm // KEY_BLOCK
    row = lambda b, i: (b, i, 0)
    const2 = lambda b, i: (0, 0)
    const3 = lambda b, i: (0, 0, 0)
    act = lambda width: pl.BlockSpec((1, tm, width), row)
    kv_f = pl.BlockSpec((1, 1, N_HEADS, HEAD_DIM, tm), lambda b, i: (0, b, 0, 0, i))
    kv_b = pl.BlockSpec((1, per_tile, D_A, KEY_BLOCK), lambda b, i: (b, i, 0, 0))
    resident = lambda a: pl.BlockSpec(a.shape, const3, pipeline_mode=pl.Buffered(1))
    return pl.pallas_call(
        _in_proj_prompt_kernel,
        grid=(bsz, t // tm),
        in_specs=[act(d), pl.BlockSpec((1, d), const2), resident(wqag), resident(wkvt)],
        out_specs=[act(D_A), kv_b, kv_b, kv_f, kv_f, act(D_C)],
        out_shape=[jax.ShapeDtypeStruct((bsz, t, D_A), BF16)]
        + [jax.ShapeDtypeStruct((bsz, t // KEY_BLOCK, D_A, KEY_BLOCK), BF16)] * 2
        + [jax.ShapeDtypeStruct((1, bsz, N_HEADS, HEAD_DIM, t), F32)] * 2
        + [jax.ShapeDtypeStruct((bsz, t, D_C), F32)],
        compiler_params=pltpu.CompilerParams(
            dimension_semantics=("arbitrary", "arbitrary"), vmem_limit_bytes=VMEM_LIMIT),
        name="in_proj_prompt",
    )(x, g1, wqag, wkvt)


def _in_proj_sample_kernel(x_ref, g1_ref, wqag_ref, wkv_ref,
                           q_ref, kb_ref, vb_ref, kf_ref, vf_ref, u_ref):
    nb, ts, d = x_ref.shape
    m = nb * ts
    x = x_ref[...].reshape(m, d)
    h = (x * _rms_scale(x) * g1_ref[...]).astype(BF16)
    q_ref[...] = (jnp.dot(h, wqag_ref[0], preferred_element_type=F32) * Q_SCALE
                  ).astype(BF16).reshape(nb, ts, D_A)
    for c, b_ref, f_ref in ((0, kb_ref, kf_ref), (1, vb_ref, vf_ref)):
        p = jnp.dot(h, wkv_ref[c], preferred_element_type=F32)
        b_ref[...] = p.astype(BF16).reshape(nb, ts, D_A)
        for hd in range(N_HEADS):
            f_ref[0, :, hd, :, :] = p[:, hd * HEAD_DIM:(hd + 1) * HEAD_DIM].reshape(nb, ts, HEAD_DIM)
    a = jnp.dot(h, wqag_ref[1], preferred_element_type=F32)
    g = jnp.dot(h, wqag_ref[2], preferred_element_type=F32)
    u_ref[...] = (a * _sigmoid(g)).reshape(nb, ts, D_C)


def _in_proj_sample(x, g1, wqag, wkv):
    bsz, ts, d = x.shape
    full = lambda shape: pl.BlockSpec(shape, lambda i: (0,) * len(shape))
    act = (bsz, ts, D_A)
    kvf = (1, bsz, N_HEADS, ts, HEAD_DIM)
    return pl.pallas_call(
        _in_proj_sample_kernel,
        grid=(1,),
        in_specs=[full(x.shape), full(g1.shape), full(wqag.shape), full(wkv.shape)],
        out_specs=[full(act), full(act), full(act), full(kvf), full(kvf), full((bsz, ts, D_C))],
        out_shape=[jax.ShapeDtypeStruct(act, BF16)] * 3
        + [jax.ShapeDtypeStruct(kvf, F32)] * 2
        + [jax.ShapeDtypeStruct((bsz, ts, D_C), F32)],
        compiler_params=pltpu.CompilerParams(
            dimension_semantics=("arbitrary",), vmem_limit_bytes=VMEM_LIMIT),
        name="in_proj_sample",
    )(x, g1, wqag, wkv)


def _softplus2(z):
    return jnp.maximum(z, 0.0) + jnp.log(1.0 + jnp.exp2(-jnp.abs(z))) * LOG2E


def _split_heads(x2):
    lane = lax.broadcasted_iota(jnp.int32, x2.shape, 1)
    zero = jnp.zeros_like(x2)
    return jnp.where(lane < HEAD_DIM, x2, zero), jnp.where(lane < HEAD_DIM, zero, x2)


def _head_norm(o, g):
    lane = lax.broadcasted_iota(jnp.int32, o.shape, 1)
    first = lane < HEAD_DIM
    sq = o * o
    ms_a = jnp.sum(jnp.where(first, sq, 0.0), axis=-1, keepdims=True)
    ms_b = jnp.sum(jnp.where(first, 0.0, sq), axis=-1, keepdims=True)
    ms = jnp.where(first, ms_a, ms_b) * (1.0 / HEAD_DIM)
    return o * lax.rsqrt(ms + EPS) * g


def _sb_prompt_kernel(q_ref, kt_ref, vt_ref, tri_ref, g_ref, o_ref, acc_ref, carry_ref):
    i = pl.program_id(1)
    tq = q_ref.shape[1]
    tri = tri_ref[...]
    row = lax.broadcasted_iota(jnp.int32, (tq, KEY_BLOCK), 0)
    col = lax.broadcasted_iota(jnp.int32, (tq, KEY_BLOCK), 1)
    diag = col < row
    qs = [_split_heads(q_ref[0, :, p * LANES:(p + 1) * LANES]) for p in range(PAIRS)]

    def visit(kb, mask):
        heads = [(p, hh) for p in range(PAIRS) for hh in range(2)]
        zs = []
        for p, hh in heads:
            kt2 = kt_ref[0, kb, p * LANES:(p + 1) * LANES, :]
            z = jnp.dot(qs[p][hh], kt2, preferred_element_type=F32)
            zs.append(z if mask is None else jnp.where(mask, z, MASKED))
        sps = [_softplus2(z).astype(BF16) for z in zs]
        locs = [jnp.dot(sp, tri, preferred_element_type=F32) for sp in sps]
        ws = [jnp.exp2(z - (loc + carry_ref[hd])).astype(BF16)
              for hd, (z, loc) in enumerate(zip(zs, locs))]
        lowest = None
        for hd, loc in enumerate(locs):
            carry = carry_ref[hd] + loc[:, :1]
            carry_ref[hd] = carry
            lowest = carry if lowest is None else jnp.minimum(lowest, carry)
        first = lax.broadcasted_iota(jnp.int32, (tq, LANES), 1) < HEAD_DIM
        for p in range(PAIRS):
            vt2 = vt_ref[0, kb, p * LANES:(p + 1) * LANES, :]
            pvs = [lax.dot_general(ws[2 * p + hh], vt2, NT, preferred_element_type=F32)
                   for hh in range(2)]
            acc_ref[p] += jnp.where(first, pvs[0], pvs[1])
        return jnp.min(lowest)

    acc_ref[...] = jnp.zeros_like(acc_ref)
    carry_ref[...] = jnp.zeros_like(carry_ref)
    lowest = visit(i, diag)

    def more(state):
        j, lowest = state
        return jnp.logical_and(j < i, lowest < UNDERFLOW_BITS)

    def body(state):
        j, _ = state
        return j + 1, visit(i - 1 - j, None)

    lax.while_loop(more, body, (jnp.int32(0), lowest))

    for p in range(PAIRS):
        sl = slice(p * LANES, (p + 1) * LANES)
        o_ref[0, :, sl] = _head_norm(acc_ref[p], g_ref[:, sl]).astype(BF16)


def _sb_prompt(q, kt, vt, tri, g):
    bsz, t, _ = q.shape
    tq = KEY_BLOCK
    q_spec = pl.BlockSpec((1, tq, D_A), lambda b, i: (b, i, 0))
    kv_spec = pl.BlockSpec((1,) + kt.shape[1:], lambda b, i: (b, 0, 0, 0))
    return pl.pallas_call(
        _sb_prompt_kernel,
        grid=(bsz, t // tq),
        in_specs=[q_spec, kv_spec, kv_spec,
                  pl.BlockSpec(tri.shape, lambda b, i: (0, 0)),
                  pl.BlockSpec((1, D_A), lambda b, i: (0, 0))],
        out_specs=q_spec,
        out_shape=jax.ShapeDtypeStruct((bsz, t, D_A), BF16),
        scratch_shapes=[pltpu.VMEM((PAIRS, tq, LANES), F32), pltpu.VMEM((N_HEADS, tq, 1), F32)],
        compiler_params=pltpu.CompilerParams(
            dimension_semantics=("arbitrary", "arbitrary"), vmem_limit_bytes=VMEM_LIMIT),
        name="sb_prompt",
    )(q, kt, vt, tri, g)


def _sb_sample_kernel(q_ref, kn_ref, vn_ref, ckt_ref, cvt_ref, tri_ref, g_ref, o_ref):
    ts = q_ref.shape[1]
    past = ckt_ref.shape[4]
    nblk = past // KEY_BLOCK
    tri = tri_ref[...]
    row = lax.broadcasted_iota(jnp.int32, (ts, ts), 0)
    col = lax.broadcasted_iota(jnp.int32, (ts, ts), 1)
    causal = col < row

    outs = []
    for hh in range(2):
        sl = slice(hh * HEAD_DIM, (hh + 1) * HEAD_DIM)
        q = q_ref[0][:, sl]
        z_n = jnp.where(causal, lax.dot_general(q, kn_ref[0][:, sl], NT, preferred_element_type=F32),
                        MASKED)
        c_n = jnp.dot(_softplus2(z_n).astype(BF16), tri[:ts, :ts], preferred_element_type=F32)
        pv = jnp.dot(jnp.exp2(z_n - c_n).astype(BF16), vn_ref[0][:, sl], preferred_element_type=F32)
        z_c = jnp.dot(q, ckt_ref[0, 0, hh].astype(BF16), preferred_element_type=F32)
        sp = _softplus2(z_c).astype(BF16)
        stacked = jnp.concatenate(
            [sp[:, b * KEY_BLOCK:(b + 1) * KEY_BLOCK] for b in range(nblk)], axis=0)
        c_loc = jnp.dot(stacked, tri, preferred_element_type=F32)
        carry = c_n[:, :1]
        cs = [None] * nblk
        for b in reversed(range(nblk)):
            loc = c_loc[b * ts:(b + 1) * ts, :]
            cs[b] = loc + carry
            carry = carry + loc[:, :1]
        w = jnp.exp2(z_c - jnp.concatenate(cs, axis=1))
        pv = pv + lax.dot_general(w.astype(BF16), cvt_ref[0, 0, hh].astype(BF16), NT,
                                  preferred_element_type=F32)
        outs.append(pv)
    o = jnp.concatenate(outs, axis=-1)
    o_ref[0] = _head_norm(o, g_ref[...]).astype(BF16)


def _sb_sample(q, kn, vn, cache_kt, cache_vt, tri, g):
    bsz, ts, _ = q.shape
    past = cache_kt.shape[4]
    new = pl.BlockSpec((1, ts, LANES), lambda b, p: (b, 0, p))
    cache = pl.BlockSpec((1, 1, 2, HEAD_DIM, past), lambda b, p: (0, b, p, 0, 0))
    return pl.pallas_call(
        _sb_sample_kernel,
        grid=(bsz, PAIRS),
        in_specs=[new, new, new, cache, cache,
                  pl.BlockSpec(tri.shape, lambda b, p: (0, 0)),
                  pl.BlockSpec((1, LANES), lambda b, p: (0, p))],
        out_specs=new,
        out_shape=jax.ShapeDtypeStruct((bsz, ts, D_A), BF16),
        compiler_params=pltpu.CompilerParams(
            dimension_semantics=("arbitrary", "arbitrary"), vmem_limit_bytes=VMEM_LIMIT),
        name="sb_sample",
    )(q, kn, vn, cache_kt, cache_vt, tri, g)


def _causal_dwconv(ext, cw_ref, cb_ref, tm):
    nb, rows, _ = ext.shape
    conv = jnp.zeros((nb, tm, D_C), F32) + cb_ref[...]
    first = HALO - CONV_HIST
    for r in range(SUBLANES):
        offsets = [m for m in range(first, first + CONV_W) if m % SUBLANES == r]
        shifted = ext if r == 0 else pltpu.roll(ext, rows - r, 1)
        for m in offsets:
            tap = m - first
            conv = conv + shifted[:, m - r:m - r + tm, :] * cw_ref[tap:tap + 1, :]
    return conv


def _conv_act(conv, lg_ref, lb_ref):
    mu = jnp.mean(conv, axis=-1, keepdims=True)
    cc = conv - mu
    ln = cc * lax.rsqrt(jnp.mean(cc * cc, axis=-1, keepdims=True) + EPS) * lg_ref[...] + lb_ref[...]
    return (ln * _sigmoid(ln)).astype(BF16)


def _mix_residual(x, o, c_act, wo_ref):
    mix = jnp.dot(o, wo_ref[0], preferred_element_type=F32)
    return x + mix + jnp.dot(c_act, wo_ref[1], preferred_element_type=F32)


def _ffn_chunk(c, hf, wg_ref, wu_ref, wd_ref, acc_ref):
    gate = jnp.dot(hf, wg_ref[c], preferred_element_type=F32)
    up = jnp.dot(hf, wu_ref[c], preferred_element_type=F32)
    act = (gate * _sigmoid(gate) * up).astype(BF16)
    acc_ref[...] += jnp.dot(act, wd_ref[c], preferred_element_type=F32)


def _out_ffn_kernel(x_ref, o_ref, u_ref, halo_ref, cw_ref, cb_ref, lg_ref, lb_ref,
                    wo_ref, g2_ref, wg_ref, wu_ref, wd_ref, gf_ref, y_ref, acc_ref):
    nb, tm, d = x_ref.shape
    m = nb * tm
    ext = jnp.concatenate([halo_ref[...], u_ref[...]], axis=1)
    c_act = _conv_act(_causal_dwconv(ext, cw_ref, cb_ref, tm).reshape(m, D_C), lg_ref, lb_ref)
    x1 = _mix_residual(x_ref[...].reshape(m, d), o_ref[...].reshape(m, D_A), c_act, wo_ref)
    hf = (x1 * _rms_scale(x1) * g2_ref[...]).astype(BF16)
    acc_ref[...] = x1

    def ffn(c, _):
        _ffn_chunk(c, hf, wg_ref, wu_ref, wd_ref, acc_ref)
        return 0

    lax.fori_loop(0, wg_ref.shape[0], ffn, 0)
    x2 = acc_ref[...]
    y_ref[...] = (x2 * _rms_scale(x2) * gf_ref[...]).reshape(nb, tm, d)


def _out_ffn_prompt_kernel(x_ref, o_ref, u_ref, halo_ref, cw_ref, cb_ref, lg_ref, lb_ref,
                           wo_ref, g2_ref, wg_ref, wu_ref, wd_ref, gf_ref, y_ref,
                           acc_ref, hf_ref, gu_ref, ext_ref, cact_ref, *, tiles_per_seq, n_tiles):
    tm, d = x_ref.shape[1:]
    g = pl.program_id(0)
    n_chunks = wg_ref.shape[0]

    @pl.when(g == 0)
    def _():
        cact_ref[...] = jnp.zeros_like(cact_ref)

    starts_seq = jnp.minimum(g, n_tiles - 1) % tiles_per_seq == 0
    ext_ref[:HALO, :] = jnp.where(starts_seq, 0.0, halo_ref[0])
    ext_ref[HALO:, :] = u_ref[0]

    x1 = _mix_residual(x_ref[0], o_ref[0], cact_ref[...], wo_ref)
    hf_ref[...] = (x1 * _rms_scale(x1) * g2_ref[...]).astype(BF16)
    acc_ref[...] = x1

    def gate_up(c, slot):
        hf = hf_ref[...]
        gu_ref[slot, 0] = jnp.dot(hf, wg_ref[c], preferred_element_type=F32)
        gu_ref[slot, 1] = jnp.dot(hf, wu_ref[c], preferred_element_type=F32)

    def down(c, slot):
        gate = gu_ref[slot, 0]
        act = (gate * _sigmoid(gate) * gu_ref[slot, 1]).astype(BF16)
        acc_ref[...] += jnp.dot(act, wd_ref[c], preferred_element_type=F32)

    def conv_piece(c):
        start = pl.multiple_of(jnp.minimum(c * CONV_PIECE, tm - CONV_PIECE), 16)
        window = ext_ref[pl.ds(start, HALO + CONV_PIECE), :]
        conv = _causal_dwconv(window[None], cw_ref, cb_ref, CONV_PIECE)[0]
        cact_ref[pl.ds(start, CONV_PIECE), :] = _conv_act(conv, lg_ref, lb_ref)

    gate_up(0, 0)

    def step(c, _):
        slot = c % 2
        down(c - 1, 1 - slot)
        gate_up(c, slot)
        conv_piece(c - 1)
        return 0

    lax.fori_loop(1, n_chunks, step, 0)
    down(n_chunks - 1, (n_chunks - 1) % 2)
    conv_piece(n_chunks - 1)
    x2 = acc_ref[...]
    y_ref[0] = x2 * _rms_scale(x2) * gf_ref[...]


def _out_ffn_prompt(x, o, u, cw, cb, lg, lb, wo2, g2, wg, wu, wd, gf):
    bsz, t, d = x.shape
    tm = ROW_TILE
    tiles_per_seq = t // tm
    n_tiles = bsz * tiles_per_seq
    assert wg.shape[0] * CONV_PIECE >= tm
    halo_per_tile = tm // HALO

    def cur(g):
        c = jnp.maximum(g - 1, 0)
        return c // tiles_per_seq, c % tiles_per_seq, 0

    def nxt(g):
        n = jnp.minimum(g, n_tiles - 1)
        return n // tiles_per_seq, n % tiles_per_seq, 0

    def nxt_halo(g):
        b, i, _ = nxt(g)
        return b, jnp.maximum(i * halo_per_tile - 1, 0), 0

    vec = lambda a: pl.BlockSpec(a.shape, lambda g: (0, 0))
    resident = lambda a: pl.BlockSpec(a.shape, lambda g: (0, 0, 0), pipeline_mode=pl.Buffered(1))
    return pl.pallas_call(
        functools.partial(_out_ffn_prompt_kernel, tiles_per_seq=tiles_per_seq, n_tiles=n_tiles),
        grid=(n_tiles + 1,),
        in_specs=[pl.BlockSpec((1, tm, d), cur),
                  pl.BlockSpec((1, tm, D_A), cur),
                  pl.BlockSpec((1, tm, D_C), nxt),
                  pl.BlockSpec((1, HALO, D_C), nxt_halo),
                  vec(cw), vec(cb), vec(lg), vec(lb),
                  resident(wo2), vec(g2), resident(wg), resident(wu), resident(wd), vec(gf)],
        out_specs=pl.BlockSpec((1, tm, d), cur),
        out_shape=jax.ShapeDtypeStruct((bsz, t, d), F32),
        scratch_shapes=[pltpu.VMEM((tm, d), F32), pltpu.VMEM((tm, d), BF16),
                        pltpu.VMEM((2, 2, tm, FF_CHUNK), F32),
                        pltpu.VMEM((HALO + tm, D_C), F32), pltpu.VMEM((tm, D_C), BF16)],
        compiler_params=pltpu.CompilerParams(
            dimension_semantics=("arbitrary",), vmem_limit_bytes=VMEM_LIMIT),
        name="out_ffn_prompt",
    )(x, o, u, u, cw, cb, lg, lb, wo2, g2, wg, wu, wd, gf)


def _out_ffn_sample(x, o, u, hist, cw, cb, lg, lb, wo2, g2, wg, wu, wd, gf):
    bsz, ts, d = x.shape
    args = (x, o, u, hist, cw, cb, lg, lb, wo2, g2, wg, wu, wd, gf)
    full = lambda a: pl.BlockSpec(a.shape, lambda i: (0,) * a.ndim)
    return pl.pallas_call(
        _out_ffn_kernel,
        grid=(1,),
        in_specs=[full(a) for a in args],
        out_specs=pl.BlockSpec(x.shape, lambda i: (0, 0, 0)),
        out_shape=jax.ShapeDtypeStruct(x.shape, F32),
        scratch_shapes=[pltpu.VMEM((bsz * ts, d), F32)],
        compiler_params=pltpu.CompilerParams(
            dimension_semantics=("arbitrary",), vmem_limit_bytes=VMEM_LIMIT),
        name="out_ffn_sample",
    )(*args)


def _row(a):
    return a.reshape(1, -1)


def kernel(x_prompt, x_sample, cache_k, cache_v, state_conv, w_in, sb_norm_g, conv_w, conv_b,
           conv_ln_g, conv_ln_b, w_out, norm1_g, norm2_g, w_gate, w_up, w_down, final_g):
    assert w_in.shape[0] == 1, "single-layer step"
    d = x_prompt.shape[-1]
    n_ff = w_gate.shape[-1] // FF_CHUNK

    w5 = w_in[0].astype(BF16).reshape(d, 5, D_A).transpose(1, 0, 2)
    wqag = jnp.stack([w5[0], w5[3], w5[4]])
    wkv = w5[1:3]
    wkvt = wkv.transpose(0, 2, 1)
    wo2 = w_out[0].astype(BF16).reshape(2, D_A, d)
    wg = w_gate[0].astype(BF16).reshape(d, n_ff, FF_CHUNK).transpose(1, 0, 2)
    wu = w_up[0].astype(BF16).reshape(d, n_ff, FF_CHUNK).transpose(1, 0, 2)
    wd = w_down[0].astype(BF16).reshape(n_ff, FF_CHUNK, d)
    g1, g2, gf = _row(norm1_g[0]), _row(norm2_g[0]), _row(final_g)
    sbg = _row(sb_norm_g[0])
    cw, cb = conv_w[0], _row(conv_b[0])
    lg, lb = _row(conv_ln_g[0]), _row(conv_ln_b[0])
    idx = jnp.arange(KEY_BLOCK)
    tri = (idx[:, None] >= idx[None, :]).astype(BF16)
    tail = (cw, cb, lg, lb, wo2, g2, wg, wu, wd, gf)

    q, ktb, vtb, ktf, vtf, u = _in_proj_prompt(x_prompt, g1, wqag, wkvt)
    o = _sb_prompt(q, ktb, vtb, tri, sbg)
    y_prompt = _out_ffn_prompt(x_prompt, o, u, *tail)
    k_prompt = jnp.swapaxes(ktf, 3, 4)
    v_prompt = jnp.swapaxes(vtf, 3, 4)
    conv_prompt = u[None, :, -CONV_HIST:, :]

    bs, ts, _ = x_sample.shape
    qs, kbs, vbs, k_sample, v_sample, us = _in_proj_sample(x_sample, g1, wqag, wkv)
    os_ = _sb_sample(qs, kbs, vbs, jnp.swapaxes(cache_k, 3, 4), jnp.swapaxes(cache_v, 3, 4), tri, sbg)
    hist = jnp.pad(state_conv[0], ((0, 0), (HALO - CONV_HIST, 0), (0, 0)))
    y_sample = _out_ffn_sample(x_sample, os_, us, hist, *tail)
    conv_sample = jnp.concatenate([state_conv[0], us], axis=1)[None, :, -CONV_HIST:, :]

    return (y_prompt, y_sample, k_prompt, v_prompt, conv_prompt, k_sample, v_sample, conv_sample)
```

```python
import jax
import jax.numpy as jnp
from jax import lax
from jax.experimental import pallas as pl
from jax.experimental.pallas import tpu as pltpu

N_HEADS = 8
HEAD_DIM = 64
D_A = N_HEADS * HEAD_DIM
D_C = 512
CONV_W = 31
CONV_HIST = CONV_W - 1
EPS = 1e-6

SUBLANES = 8
LANES = 128
PAIRS = D_A // LANES
HALO = 32
KEY_BLOCK = 256
FF_CHUNK = 256
ROW_TILE = 512
VMEM_LIMIT = 56 * 1024 * 1024

LOG2E = 1.4426950408889634
Q_SCALE = HEAD_DIM ** -0.5 * LOG2E
MASKED = -1e30
UNDERFLOW_BITS = 150.0
SOFTPLUS_SATURATION = 100.0

BF16 = jnp.bfloat16
F32 = jnp.float32

NT = (((1,), (1,)), ((), ()))


def _rms_scale(x):
    return lax.rsqrt(jnp.mean(x * x, axis=-1, keepdims=True) + EPS)


def _sigmoid(x):
    return 0.5 + 0.5 * jnp.tanh(0.5 * x)


def _causal_dwconv(ext, cw_ref, cb_ref, tm):
    nb, rows, _ = ext.shape
    conv = jnp.zeros((nb, tm, D_C), F32) + cb_ref[...]
    first = HALO - CONV_HIST
    for r in range(SUBLANES):
        offsets = [m for m in range(first, first + CONV_W) if m % SUBLANES == r]
        shifted = ext if r == 0 else pltpu.roll(ext, rows - r, 1)
        for m in offsets:
            tap = m - first
            conv = conv + shifted[:, m - r:m - r + tm, :] * cw_ref[tap:tap + 1, :]
    return conv


def _conv_act(conv, lg_ref, lb_ref):
    mu = jnp.mean(conv, axis=-1, keepdims=True)
    cc = conv - mu
    ln = cc * lax.rsqrt(jnp.mean(cc * cc, axis=-1, keepdims=True) + EPS) * lg_ref[...] + lb_ref[...]
    return (ln * _sigmoid(ln)).astype(BF16)


def _in_proj_prompt_kernel(x_ref, g1_ref, wqag_ref, wkvt_ref, cw_ref, cb_ref, lg_ref, lb_ref,
                           q_ref, kb_ref, vb_ref, kf_ref, vf_ref, c_ref, tail_ref, hist_ref):
    tm = x_ref.shape[1]

    @pl.when(pl.program_id(1) == 0)
    def _():
        hist_ref[...] = jnp.zeros_like(hist_ref)

    x = x_ref[0]
    h = (x * _rms_scale(x) * g1_ref[...]).astype(BF16)
    a = jnp.dot(h, wqag_ref[1], preferred_element_type=F32)
    g = jnp.dot(h, wqag_ref[2], preferred_element_type=F32)
    u = a * _sigmoid(g)
    ext = jnp.concatenate([hist_ref[...], u], axis=0)
    c_ref[0] = _conv_act(_causal_dwconv(ext[None], cw_ref, cb_ref, tm)[0], lg_ref, lb_ref)
    hist_ref[...] = u[tm - HALO:, :]
    tail_ref[0] = u[tm - HALO:, :]

    q_ref[0] = (jnp.dot(h, wqag_ref[0], preferred_element_type=F32) * Q_SCALE).astype(BF16)
    for c, b_ref, f_ref in ((0, kb_ref, kf_ref), (1, vb_ref, vf_ref)):
        pt = lax.dot_general(wkvt_ref[c], h, NT, preferred_element_type=F32)
        f_ref[0, 0] = pt.reshape(N_HEADS, HEAD_DIM, tm)
        for j in range(tm // KEY_BLOCK):
            b_ref[0, j] = pt[:, j * KEY_BLOCK:(j + 1) * KEY_BLOCK].astype(BF16)


def _in_proj_prompt(x, g1, wqag, wkvt, cw, cb, lg, lb):
    bsz, t, d = x.shape
    tm = ROW_TILE
    per_tile = tm // KEY_BLOCK
    row = lambda b, i: (b, i, 0)
    const2 = lambda b, i: (0, 0)
    const3 = lambda b, i: (0, 0, 0)
    act = lambda width: pl.BlockSpec((1, tm, width), row)
    vec = lambda a: pl.BlockSpec(a.shape, const2)
    kv_f = pl.BlockSpec((1, 1, N_HEADS, HEAD_DIM, tm), lambda b, i: (0, b, 0, 0, i))
    kv_b = pl.BlockSpec((1, per_tile, D_A, KEY_BLOCK), lambda b, i: (b, i, 0, 0))
    resident = lambda a: pl.BlockSpec(a.shape, const3, pipeline_mode=pl.Buffered(1))
    return pl.pallas_call(
        _in_proj_prompt_kernel,
        grid=(bsz, t // tm),
        in_specs=[act(d), vec(g1), resident(wqag), resident(wkvt), vec(cw), vec(cb), vec(lg), vec(lb)],
        out_specs=[act(D_A), kv_b, kv_b, kv_f, kv_f, act(D_C),
                   pl.BlockSpec((1, HALO, D_C), lambda b, i: (b, 0, 0))],
        out_shape=[jax.ShapeDtypeStruct((bsz, t, D_A), BF16)]
        + [jax.ShapeDtypeStruct((bsz, t // KEY_BLOCK, D_A, KEY_BLOCK), BF16)] * 2
        + [jax.ShapeDtypeStruct((1, bsz, N_HEADS, HEAD_DIM, t), F32)] * 2
        + [jax.ShapeDtypeStruct((bsz, t, D_C), BF16), jax.ShapeDtypeStruct((bsz, HALO, D_C), F32)],
        scratch_shapes=[pltpu.VMEM((HALO, D_C), F32)],
        compiler_params=pltpu.CompilerParams(
            dimension_semantics=("arbitrary", "arbitrary"), vmem_limit_bytes=VMEM_LIMIT),
        name="in_proj_prompt",
    )(x, g1, wqag, wkvt, cw, cb, lg, lb)


def _in_proj_sample_kernel(x_ref, g1_ref, wqag_ref, wkv_ref, hist_ref, cw_ref, cb_ref, lg_ref, lb_ref,
                           q_ref, kb_ref, vb_ref, kf_ref, vf_ref, c_ref, u_ref):
    nb, ts, d = x_ref.shape
    m = nb * ts
    x = x_ref[...].reshape(m, d)
    h = (x * _rms_scale(x) * g1_ref[...]).astype(BF16)
    q_ref[...] = (jnp.dot(h, wqag_ref[0], preferred_element_type=F32) * Q_SCALE
                  ).astype(BF16).reshape(nb, ts, D_A)
    for c, b_ref, f_ref in ((0, kb_ref, kf_ref), (1, vb_ref, vf_ref)):
        p = jnp.dot(h, wkv_ref[c], preferred_element_type=F32)
        b_ref[...] = p.astype(BF16).reshape(nb, ts, D_A)
        for hd in range(N_HEADS):
            f_ref[0, :, hd, :, :] = p[:, hd * HEAD_DIM:(hd + 1) * HEAD_DIM].reshape(nb, ts, HEAD_DIM)
    a = jnp.dot(h, wqag_ref[1], preferred_element_type=F32)
    g = jnp.dot(h, wqag_ref[2], preferred_element_type=F32)
    u = (a * _sigmoid(g)).reshape(nb, ts, D_C)
    u_ref[...] = u
    ext = jnp.concatenate([hist_ref[...], u], axis=1)
    conv = _causal_dwconv(ext, cw_ref, cb_ref, ts).reshape(m, D_C)
    c_ref[...] = _conv_act(conv, lg_ref, lb_ref).reshape(nb, ts, D_C)


def _in_proj_sample(x, g1, wqag, wkv, hist, cw, cb, lg, lb):
    bsz, ts, d = x.shape
    args = (x, g1, wqag, wkv, hist, cw, cb, lg, lb)
    full = lambda shape: pl.BlockSpec(shape, lambda i: (0,) * len(shape))
    act = (bsz, ts, D_A)
    kvf = (1, bsz, N_HEADS, ts, HEAD_DIM)
    return pl.pallas_call(
        _in_proj_sample_kernel,
        grid=(1,),
        in_specs=[full(a.shape) for a in args],
        out_specs=[full(act), full(act), full(act), full(kvf), full(kvf), full(act), full(act)],
        out_shape=[jax.ShapeDtypeStruct(act, BF16)] * 3
        + [jax.ShapeDtypeStruct(kvf, F32)] * 2
        + [jax.ShapeDtypeStruct((bsz, ts, D_C), BF16), jax.ShapeDtypeStruct((bsz, ts, D_C), F32)],
        compiler_params=pltpu.CompilerParams(
            dimension_semantics=("arbitrary",), vmem_limit_bytes=VMEM_LIMIT),
        name="in_proj_sample",
    )(*args)


def _softplus2(z):
    return jnp.maximum(jnp.log(1.0 + jnp.exp2(jnp.minimum(z, SOFTPLUS_SATURATION))) * LOG2E, z)


def _split_heads(x2):
    lane = lax.broadcasted_iota(jnp.int32, x2.shape, 1)
    zero = jnp.zeros_like(x2)
    return jnp.where(lane < HEAD_DIM, x2, zero), jnp.where(lane < HEAD_DIM, zero, x2)


def _head_norm(o, g):
    lane = lax.broadcasted_iota(jnp.int32, o.shape, 1)
    first = lane < HEAD_DIM
    sq = o * o
    ms_a = jnp.sum(jnp.where(first, sq, 0.0), axis=-1, keepdims=True)
    ms_b = jnp.sum(jnp.where(first, 0.0, sq), axis=-1, keepdims=True)
    ms = jnp.where(first, ms_a, ms_b) * (1.0 / HEAD_DIM)
    return o * lax.rsqrt(ms + EPS) * g


def _sb_prompt_kernel(q_ref, kt_ref, vt_ref, tri_ref, g_ref, o_ref, acc_ref, carry_ref):
    i = pl.program_id(1)
    tq = q_ref.shape[1]
    tri = tri_ref[...]
    row = lax.broadcasted_iota(jnp.int32, (tq, KEY_BLOCK), 0)
    col = lax.broadcasted_iota(jnp.int32, (tq, KEY_BLOCK), 1)
    diag = col < row
    qs = [_split_heads(q_ref[0, :, p * LANES:(p + 1) * LANES]) for p in range(PAIRS)]

    def visit(kb, mask):
        heads = [(p, hh) for p in range(PAIRS) for hh in range(2)]
        zs = []
        for p, hh in heads:
            kt2 = kt_ref[0, kb, p * LANES:(p + 1) * LANES, :]
            z = jnp.dot(qs[p][hh], kt2, preferred_element_type=F32)
            zs.append(z if mask is None else jnp.where(mask, z, MASKED))
        sps = [_softplus2(z).astype(BF16) for z in zs]
        locs = [jnp.dot(sp, tri, preferred_element_type=F32) for sp in sps]
        ws = [jnp.exp2(z - (loc + carry_ref[hd])).astype(BF16)
              for hd, (z, loc) in enumerate(zip(zs, locs))]
        lowest = None
        for hd, loc in enumerate(locs):
            carry = carry_ref[hd] + loc[:, :1]
            carry_ref[hd] = carry
            lowest = carry if lowest is None else jnp.minimum(lowest, carry)
        first = lax.broadcasted_iota(jnp.int32, (tq, LANES), 1) < HEAD_DIM
        for p in range(PAIRS):
            vt2 = vt_ref[0, kb, p * LANES:(p + 1) * LANES, :]
            pvs = [lax.dot_general(ws[2 * p + hh], vt2, NT, preferred_element_type=F32)
                   for hh in range(2)]
            acc_ref[p] += jnp.where(first, pvs[0], pvs[1])
        return jnp.min(lowest)

    acc_ref[...] = jnp.zeros_like(acc_ref)
    carry_ref[...] = jnp.zeros_like(carry_ref)
    lowest = visit(i, diag)

    def more(state):
        j, lowest = state
        return jnp.logical_and(j < i, lowest < UNDERFLOW_BITS)

    def body(state):
        j, _ = state
        return j + 1, visit(i - 1 - j, None)

    lax.while_loop(more, body, (jnp.int32(0), lowest))

    for p in range(PAIRS):
        sl = slice(p * LANES, (p + 1) * LANES)
        o_ref[0, :, sl] = _head_norm(acc_ref[p], g_ref[:, sl]).astype(BF16)


def _sb_prompt(q, kt, vt, tri, g):
    bsz, t, _ = q.shape
    tq = KEY_BLOCK
    q_spec = pl.BlockSpec((1, tq, D_A), lambda b, i: (b, i, 0))
    kv_spec = pl.BlockSpec((1,) + kt.shape[1:], lambda b, i: (b, 0, 0, 0))
    return pl.pallas_call(
        _sb_prompt_kernel,
        grid=(bsz, t // tq),
        in_specs=[q_spec, kv_spec, kv_spec,
                  pl.BlockSpec(tri.shape, lambda b, i: (0, 0)),
                  pl.BlockSpec((1, D_A), lambda b, i: (0, 0))],
        out_specs=q_spec,
        out_shape=jax.ShapeDtypeStruct((bsz, t, D_A), BF16),
        scratch_shapes=[pltpu.VMEM((PAIRS, tq, LANES), F32), pltpu.VMEM((N_HEADS, tq, 1), F32)],
        compiler_params=pltpu.CompilerParams(
            dimension_semantics=("arbitrary", "arbitrary"), vmem_limit_bytes=VMEM_LIMIT),
        name="sb_prompt",
    )(q, kt, vt, tri, g)


def _sb_sample_kernel(q_ref, kn_ref, vn_ref, ckt_ref, cvt_ref, tri_ref, g_ref, o_ref):
    ts = q_ref.shape[1]
    past = ckt_ref.shape[4]
    nblk = past // KEY_BLOCK
    tri = tri_ref[...]
    row = lax.broadcasted_iota(jnp.int32, (ts, ts), 0)
    col = lax.broadcasted_iota(jnp.int32, (ts, ts), 1)
    causal = col < row

    outs = []
    for hh in range(2):
        sl = slice(hh * HEAD_DIM, (hh + 1) * HEAD_DIM)
        q = q_ref[0][:, sl]
        z_n = jnp.where(causal, lax.dot_general(q, kn_ref[0][:, sl], NT, preferred_element_type=F32),
                        MASKED)
        c_n = jnp.dot(_softplus2(z_n).astype(BF16), tri[:ts, :ts], preferred_element_type=F32)
        pv = jnp.dot(jnp.exp2(z_n - c_n).astype(BF16), vn_ref[0][:, sl], preferred_element_type=F32)
        z_c = jnp.dot(q, ckt_ref[0, 0, hh].astype(BF16), preferred_element_type=F32)
        sp = _softplus2(z_c).astype(BF16)
        stacked = jnp.concatenate(
            [sp[:, b * KEY_BLOCK:(b + 1) * KEY_BLOCK] for b in range(nblk)], axis=0)
        c_loc = jnp.dot(stacked, tri, preferred_element_type=F32)
        carry = c_n[:, :1]
        cs = [None] * nblk
        for b in reversed(range(nblk)):
            loc = c_loc[b * ts:(b + 1) * ts, :]
            cs[b] = loc + carry
            carry = carry + loc[:, :1]
        w = jnp.exp2(z_c - jnp.concatenate(cs, axis=1))
        pv = pv + lax.dot_general(w.astype(BF16), cvt_ref[0, 0, hh].astype(BF16), NT,
                                  preferred_element_type=F32)
        outs.append(pv)
    o = jnp.concatenate(outs, axis=-1)
    o_ref[0] = _head_norm(o, g_ref[...]).astype(BF16)


def _sb_sample(q, kn, vn, cache_kt, cache_vt, tri, g):
    bsz, ts, _ = q.shape
    past = cache_kt.shape[4]
    new = pl.BlockSpec((1, ts, LANES), lambda b, p: (b, 0, p))
    cache = pl.BlockSpec((1, 1, 2, HEAD_DIM, past), lambda b, p: (0, b, p, 0, 0))
    return pl.pallas_call(
        _sb_sample_kernel,
        grid=(bsz, PAIRS),
        in_specs=[new, new, new, cache, cache,
                  pl.BlockSpec(tri.shape, lambda b, p: (0, 0)),
                  pl.BlockSpec((1, LANES), lambda b, p: (0, p))],
        out_specs=new,
        out_shape=jax.ShapeDtypeStruct((bsz, ts, D_A), BF16),
        compiler_params=pltpu.CompilerParams(
            dimension_semantics=("arbitrary", "arbitrary"), vmem_limit_bytes=VMEM_LIMIT),
        name="sb_sample",
    )(q, kn, vn, cache_kt, cache_vt, tri, g)


def _out_ffn_kernel(x_ref, o_ref, c_ref, wo_ref, g2_ref, wg_ref, wu_ref, wd_ref, gf_ref, y_ref,
                    acc_ref):
    nb, tm, d = x_ref.shape
    m = nb * tm
    mix = jnp.dot(o_ref[...].reshape(m, D_A), wo_ref[0], preferred_element_type=F32)
    mix = mix + jnp.dot(c_ref[...].reshape(m, D_C), wo_ref[1], preferred_element_type=F32)
    x1 = x_ref[...].reshape(m, d) + mix
    hf = (x1 * _rms_scale(x1) * g2_ref[...]).astype(BF16)
    acc_ref[...] = x1

    def ffn(c, _):
        gate = jnp.dot(hf, wg_ref[c], preferred_element_type=F32)
        up = jnp.dot(hf, wu_ref[c], preferred_element_type=F32)
        act = (gate * _sigmoid(gate) * up).astype(BF16)
        acc_ref[...] += jnp.dot(act, wd_ref[c], preferred_element_type=F32)
        return 0

    lax.fori_loop(0, wg_ref.shape[0], ffn, 0)
    x2 = acc_ref[...]
    y_ref[...] = (x2 * _rms_scale(x2) * gf_ref[...]).reshape(nb, tm, d)


def _out_ffn(x, o, c, wo2, g2, wg, wu, wd, gf, nb, tm):
    bsz, t, d = x.shape
    row = lambda b, i: (b, i, 0)
    act = lambda width: pl.BlockSpec((nb, tm, width), row)
    vec = lambda a: pl.BlockSpec(a.shape, lambda b, i: (0, 0))
    resident = lambda a: pl.BlockSpec(a.shape, lambda b, i: (0, 0, 0), pipeline_mode=pl.Buffered(1))
    return pl.pallas_call(
        _out_ffn_kernel,
        grid=(bsz // nb, t // tm),
        in_specs=[act(d), act(D_A), act(D_C),
                  resident(wo2), vec(g2), resident(wg), resident(wu), resident(wd), vec(gf)],
        out_specs=act(d),
        out_shape=jax.ShapeDtypeStruct((bsz, t, d), F32),
        scratch_shapes=[pltpu.VMEM((nb * tm, d), F32)],
        compiler_params=pltpu.CompilerParams(
            dimension_semantics=("arbitrary", "arbitrary"), vmem_limit_bytes=VMEM_LIMIT),
        name="out_ffn",
    )(x, o, c, wo2, g2, wg, wu, wd, gf)


def _row(a):
    return a.reshape(1, -1)


def kernel(x_prompt, x_sample, cache_k, cache_v, state_conv, w_in, sb_norm_g, conv_w, conv_b,
           conv_ln_g, conv_ln_b, w_out, norm1_g, norm2_g, w_gate, w_up, w_down, final_g):
    assert w_in.shape[0] == 1, "single-layer step"
    d = x_prompt.shape[-1]
    n_ff = w_gate.shape[-1] // FF_CHUNK

    w5 = w_in[0].astype(BF16).reshape(d, 5, D_A).transpose(1, 0, 2)
    wqag = jnp.stack([w5[0], w5[3], w5[4]])
    wkv = w5[1:3]
    wkvt = wkv.transpose(0, 2, 1)
    wo2 = w_out[0].astype(BF16).reshape(2, D_A, d)
    wg = w_gate[0].astype(BF16).reshape(d, n_ff, FF_CHUNK).transpose(1, 0, 2)
    wu = w_up[0].astype(BF16).reshape(d, n_ff, FF_CHUNK).transpose(1, 0, 2)
    wd = w_down[0].astype(BF16).reshape(n_ff, FF_CHUNK, d)
    g1, g2, gf = _row(norm1_g[0]), _row(norm2_g[0]), _row(final_g)
    sbg = _row(sb_norm_g[0])
    cw, cb = conv_w[0], _row(conv_b[0])
    lg, lb = _row(conv_ln_g[0]), _row(conv_ln_b[0])
    idx = jnp.arange(KEY_BLOCK)
    tri = (idx[:, None] >= idx[None, :]).astype(BF16)
    conv_params = (cw, cb, lg, lb)
    tail = (wo2, g2, wg, wu, wd, gf)

    q, ktb, vtb, ktf, vtf, c, u_tail = _in_proj_prompt(x_prompt, g1, wqag, wkvt, *conv_params)
    o = _sb_prompt(q, ktb, vtb, tri, sbg)
    y_prompt = _out_ffn(x_prompt, o, c, *tail, 1, ROW_TILE)
    k_prompt = jnp.swapaxes(ktf, 3, 4)
    v_prompt = jnp.swapaxes(vtf, 3, 4)
    conv_prompt = u_tail[None, :, HALO - CONV_HIST:, :]

    bs, ts, _ = x_sample.shape
    hist = jnp.pad(state_conv[0], ((0, 0), (HALO - CONV_HIST, 0), (0, 0)))
    qs, kbs, vbs, k_sample, v_sample, cs, us = _in_proj_sample(x_sample, g1, wqag, wkv, hist, *conv_params)
    os_ = _sb_sample(qs, kbs, vbs, jnp.swapaxes(cache_k, 3, 4), jnp.swapaxes(cache_v, 3, 4), tri, sbg)
    y_sample = _out_ffn(x_sample, os_, cs, *tail, bs, ts)
    conv_sample = jnp.concatenate([state_conv[0], us], axis=1)[None, :, -CONV_HIST:, :]

    return (y_prompt, y_sample, k_prompt, v_prompt, conv_prompt, k_sample, v_sample, conv_sample)
```

```python
import jax
import jax.numpy as jnp
from jax import lax
from jax.experimental import pallas as pl
from jax.experimental.pallas import tpu as pltpu

N_HEADS = 8
HEAD_DIM = 64
D_A = N_HEADS * HEAD_DIM
D_C = 512
CONV_W = 31
CONV_HIST = CONV_W - 1
EPS = 1e-6

SUBLANES = 8
LANES = 128
PAIRS = D_A // LANES
HALO = 32
KEY_BLOCK = 256
FF_CHUNK = 256
ROW_TILE = 512
VMEM_LIMIT = 56 * 1024 * 1024

LOG2E = 1.4426950408889634
Q_SCALE = HEAD_DIM ** -0.5 * LOG2E
MASKED = -1e30
UNDERFLOW_BITS = 150.0
SOFTPLUS_SATURATION = 100.0

BF16 = jnp.bfloat16
F32 = jnp.float32

NT = (((1,), (1,)), ((), ()))


def _rms_scale(x):
    return lax.rsqrt(jnp.mean(x * x, axis=-1, keepdims=True) + EPS)


def _sigmoid(x):
    return 0.5 + 0.5 * jnp.tanh(0.5 * x)


def _causal_dwconv(ext, cw_ref, cb_ref, tm):
    nb, rows, _ = ext.shape
    conv = jnp.zeros((nb, tm, D_C), F32) + cb_ref[...]
    first = HALO - CONV_HIST
    for r in range(SUBLANES):
        offsets = [m for m in range(first, first + CONV_W) if m % SUBLANES == r]
        shifted = ext if r == 0 else pltpu.roll(ext, rows - r, 1)
        for m in offsets:
            tap = m - first
            conv = conv + shifted[:, m - r:m - r + tm, :] * cw_ref[tap:tap + 1, :]
    return conv


def _conv_act(conv, lg_ref, lb_ref):
    mu = jnp.mean(conv, axis=-1, keepdims=True)
    cc = conv - mu
    ln = cc * lax.rsqrt(jnp.mean(cc * cc, axis=-1, keepdims=True) + EPS) * lg_ref[...] + lb_ref[...]
    return (ln * _sigmoid(ln)).astype(BF16)


def _in_proj_prompt_kernel(x_ref, g1_ref, wqag_ref, wkvt_ref, cw_ref, cb_ref, lg_ref, lb_ref,
                           q_ref, kb_ref, vb_ref, kf_ref, vf_ref, c_ref, tail_ref, hist_ref):
    tm = x_ref.shape[1]

    @pl.when(pl.program_id(1) == 0)
    def _():
        hist_ref[...] = jnp.zeros_like(hist_ref)

    x = x_ref[0]
    h = (x * _rms_scale(x) * g1_ref[...]).astype(BF16)
    a = jnp.dot(h, wqag_ref[1], preferred_element_type=F32)
    g = jnp.dot(h, wqag_ref[2], preferred_element_type=F32)
    u = a * _sigmoid(g)
    ext = jnp.concatenate([hist_ref[...], u], axis=0)
    c_ref[0] = _conv_act(_causal_dwconv(ext[None], cw_ref, cb_ref, tm)[0], lg_ref, lb_ref)
    hist_ref[...] = u[tm - HALO:, :]
    tail_ref[0] = u[tm - HALO:, :]

    q_ref[0] = (jnp.dot(h, wqag_ref[0], preferred_element_type=F32) * Q_SCALE).astype(BF16)
    for c, b_ref, f_ref in ((0, kb_ref, kf_ref), (1, vb_ref, vf_ref)):
        pt = lax.dot_general(wkvt_ref[c], h, NT, preferred_element_type=F32)
        f_ref[0, 0] = pt.reshape(N_HEADS, HEAD_DIM, tm)
        for j in range(tm // KEY_BLOCK):
            b_ref[0, j] = pt[:, j * KEY_BLOCK:(j + 1) * KEY_BLOCK].astype(BF16)


def _in_proj_prompt(x, g1, wqag, wkvt, cw, cb, lg, lb):
    bsz, t, d = x.shape
    tm = ROW_TILE
    per_tile = tm // KEY_BLOCK
    row = lambda b, i: (b, i, 0)
    const2 = lambda b, i: (0, 0)
    const3 = lambda b, i: (0, 0, 0)
    act = lambda width: pl.BlockSpec((1, tm, width), row)
    vec = lambda a: pl.BlockSpec(a.shape, const2)
    kv_f = pl.BlockSpec((1, 1, N_HEADS, HEAD_DIM, tm), lambda b, i: (0, b, 0, 0, i))
    kv_b = pl.BlockSpec((1, per_tile, D_A, KEY_BLOCK), lambda b, i: (b, i, 0, 0))
    resident = lambda a: pl.BlockSpec(a.shape, const3, pipeline_mode=pl.Buffered(1))
    return pl.pallas_call(
        _in_proj_prompt_kernel,
        grid=(bsz, t // tm),
        in_specs=[act(d), vec(g1), resident(wqag), resident(wkvt), vec(cw), vec(cb), vec(lg), vec(lb)],
        out_specs=[act(D_A), kv_b, kv_b, kv_f, kv_f, act(D_C),
                   pl.BlockSpec((1, HALO, D_C), lambda b, i: (b, 0, 0))],
        out_shape=[jax.ShapeDtypeStruct((bsz, t, D_A), BF16)]
        + [jax.ShapeDtypeStruct((bsz, t // KEY_BLOCK, D_A, KEY_BLOCK), BF16)] * 2
        + [jax.ShapeDtypeStruct((1, bsz, N_HEADS, HEAD_DIM, t), F32)] * 2
        + [jax.ShapeDtypeStruct((bsz, t, D_C), BF16), jax.ShapeDtypeStruct((bsz, HALO, D_C), F32)],
        scratch_shapes=[pltpu.VMEM((HALO, D_C), F32)],
        compiler_params=pltpu.CompilerParams(
            dimension_semantics=("arbitrary", "arbitrary"), vmem_limit_bytes=VMEM_LIMIT),
        name="in_proj_prompt",
    )(x, g1, wqag, wkvt, cw, cb, lg, lb)


def _in_proj_sample_kernel(x_ref, g1_ref, wqag_ref, wkv_ref, hist_ref, cw_ref, cb_ref, lg_ref, lb_ref,
                           q_ref, kb_ref, vb_ref, kf_ref, vf_ref, c_ref, u_ref):
    nb, ts, d = x_ref.shape
    m = nb * ts
    x = x_ref[...].reshape(m, d)
    h = (x * _rms_scale(x) * g1_ref[...]).astype(BF16)
    q_ref[...] = (jnp.dot(h, wqag_ref[0], preferred_element_type=F32) * Q_SCALE
                  ).astype(BF16).reshape(nb, ts, D_A)
    for c, b_ref, f_ref in ((0, kb_ref, kf_ref), (1, vb_ref, vf_ref)):
        p = jnp.dot(h, wkv_ref[c], preferred_element_type=F32)
        b_ref[...] = p.astype(BF16).reshape(nb, ts, D_A)
        for hd in range(N_HEADS):
            f_ref[0, :, hd, :, :] = p[:, hd * HEAD_DIM:(hd + 1) * HEAD_DIM].reshape(nb, ts, HEAD_DIM)
    a = jnp.dot(h, wqag_ref[1], preferred_element_type=F32)
    g = jnp.dot(h, wqag_ref[2], preferred_element_type=F32)
    u = (a * _sigmoid(g)).reshape(nb, ts, D_C)
    u_ref[...] = u
    ext = jnp.concatenate([hist_ref[...], u], axis=1)
    conv = _causal_dwconv(ext, cw_ref, cb_ref, ts).reshape(m, D_C)
    c_ref[...] = _conv_act(conv, lg_ref, lb_ref).reshape(nb, ts, D_C)


def _in_proj_sample(x, g1, wqag, wkv, hist, cw, cb, lg, lb):
    bsz, ts, d = x.shape
    args = (x, g1, wqag, wkv, hist, cw, cb, lg, lb)
    full = lambda shape: pl.BlockSpec(shape, lambda i: (0,) * len(shape))
    act = (bsz, ts, D_A)
    kvf = (1, bsz, N_HEADS, ts, HEAD_DIM)
    return pl.pallas_call(
        _in_proj_sample_kernel,
        grid=(1,),
        in_specs=[full(a.shape) for a in args],
        out_specs=[full(act), full(act), full(act), full(kvf), full(kvf), full(act), full(act)],
        out_shape=[jax.ShapeDtypeStruct(act, BF16)] * 3
        + [jax.ShapeDtypeStruct(kvf, F32)] * 2
        + [jax.ShapeDtypeStruct((bsz, ts, D_C), BF16), jax.ShapeDtypeStruct((bsz, ts, D_C), F32)],
        compiler_params=pltpu.CompilerParams(
            dimension_semantics=("arbitrary",), vmem_limit_bytes=VMEM_LIMIT),
        name="in_proj_sample",
    )(*args)


def _softplus2(z):
    return jnp.maximum(jnp.log(1.0 + jnp.exp2(jnp.minimum(z, SOFTPLUS_SATURATION))) * LOG2E, z)


def _split_heads(x2):
    lane = lax.broadcasted_iota(jnp.int32, x2.shape, 1)
    zero = jnp.zeros_like(x2)
    return jnp.where(lane < HEAD_DIM, x2, zero), jnp.where(lane < HEAD_DIM, zero, x2)


def _head_norm(o, g):
    lane = lax.broadcasted_iota(jnp.int32, o.shape, 1)
    first = lane < HEAD_DIM
    sq = o * o
    ms_a = jnp.sum(jnp.where(first, sq, 0.0), axis=-1, keepdims=True)
    ms_b = jnp.sum(jnp.where(first, 0.0, sq), axis=-1, keepdims=True)
    ms = jnp.where(first, ms_a, ms_b) * (1.0 / HEAD_DIM)
    return o * lax.rsqrt(ms + EPS) * g


def _sb_prompt_kernel(q_ref, kt_ref, vt_ref, tri_ref, g_ref, o_ref, acc_ref, carry_ref):
    i = pl.program_id(1)
    tq = q_ref.shape[1]
    tri = tri_ref[...]
    row = lax.broadcasted_iota(jnp.int32, (tq, KEY_BLOCK), 0)
    col = lax.broadcasted_iota(jnp.int32, (tq, KEY_BLOCK), 1)
    diag = col < row
    qs = [_split_heads(q_ref[0, :, p * LANES:(p + 1) * LANES]) for p in range(PAIRS)]

    def visit(blocks):
        heads = [(p, hh) for p in range(PAIRS) for hh in range(2)]
        zs = {}
        for n, (kb, mask) in enumerate(blocks):
            for hd, (p, hh) in enumerate(heads):
                kt2 = kt_ref[0, kb, p * LANES:(p + 1) * LANES, :]
                z = jnp.dot(qs[p][hh], kt2, preferred_element_type=F32)
                zs[n, hd] = z if mask is None else jnp.where(mask, z, MASKED)
        sps = {key: _softplus2(z).astype(BF16) for key, z in zs.items()}
        locs = {key: jnp.dot(sp, tri, preferred_element_type=F32) for key, sp in sps.items()}
        ws = {}
        lowest = None
        for hd in range(N_HEADS):
            carry = carry_ref[hd]
            for n in range(len(blocks)):
                ws[n, hd] = jnp.exp2(zs[n, hd] - (locs[n, hd] + carry)).astype(BF16)
                carry = carry + locs[n, hd][:, :1]
            carry_ref[hd] = carry
            lowest = carry if lowest is None else jnp.minimum(lowest, carry)
        first = lax.broadcasted_iota(jnp.int32, (tq, LANES), 1) < HEAD_DIM
        for p in range(PAIRS):
            pair = None
            for n, (kb, _) in enumerate(blocks):
                vt2 = vt_ref[0, kb, p * LANES:(p + 1) * LANES, :]
                pvs = [lax.dot_general(ws[n, 2 * p + hh], vt2, NT, preferred_element_type=F32)
                       for hh in range(2)]
                part = jnp.where(first, pvs[0], pvs[1])
                pair = part if pair is None else pair + part
            acc_ref[p] += pair
        return jnp.min(lowest)

    acc_ref[...] = jnp.zeros_like(acc_ref)
    carry_ref[...] = jnp.zeros_like(carry_ref)
    lowest = lax.cond(i == 0,
                      lambda: visit([(i, diag)]),
                      lambda: visit([(i, diag), (i - 1, None)]))

    def more(state):
        j, lowest = state
        return jnp.logical_and(j < i, lowest < UNDERFLOW_BITS)

    def body(state):
        j, _ = state
        return j + 1, visit([(i - 1 - j, None)])

    lax.while_loop(more, body, (jnp.int32(1), lowest))

    for p in range(PAIRS):
        sl = slice(p * LANES, (p + 1) * LANES)
        o_ref[0, :, sl] = _head_norm(acc_ref[p], g_ref[:, sl]).astype(BF16)


def _sb_prompt(q, kt, vt, tri, g):
    bsz, t, _ = q.shape
    tq = KEY_BLOCK
    q_spec = pl.BlockSpec((1, tq, D_A), lambda b, i: (b, i, 0))
    kv_spec = pl.BlockSpec((1,) + kt.shape[1:], lambda b, i: (b, 0, 0, 0))
    return pl.pallas_call(
        _sb_prompt_kernel,
        grid=(bsz, t // tq),
        in_specs=[q_spec, kv_spec, kv_spec,
                  pl.BlockSpec(tri.shape, lambda b, i: (0, 0)),
                  pl.BlockSpec((1, D_A), lambda b, i: (0, 0))],
        out_specs=q_spec,
        out_shape=jax.ShapeDtypeStruct((bsz, t, D_A), BF16),
        scratch_shapes=[pltpu.VMEM((PAIRS, tq, LANES), F32), pltpu.VMEM((N_HEADS, tq, 1), F32)],
        compiler_params=pltpu.CompilerParams(
            dimension_semantics=("arbitrary", "arbitrary"), vmem_limit_bytes=VMEM_LIMIT),
        name="sb_prompt",
    )(q, kt, vt, tri, g)


def _sb_sample_kernel(q_ref, kn_ref, vn_ref, ckt_ref, cvt_ref, tri_ref, g_ref, o_ref):
    ts = q_ref.shape[1]
    past = ckt_ref.shape[4]
    nblk = past // KEY_BLOCK
    tri = tri_ref[...]
    row = lax.broadcasted_iota(jnp.int32, (ts, ts), 0)
    col = lax.broadcasted_iota(jnp.int32, (ts, ts), 1)
    causal = col < row

    outs = []
    for hh in range(2):
        sl = slice(hh * HEAD_DIM, (hh + 1) * HEAD_DIM)
        q = q_ref[0][:, sl]
        z_n = jnp.where(causal, lax.dot_general(q, kn_ref[0][:, sl], NT, preferred_element_type=F32),
                        MASKED)
        c_n = jnp.dot(_softplus2(z_n).astype(BF16), tri[:ts, :ts], preferred_element_type=F32)
        pv = jnp.dot(jnp.exp2(z_n - c_n).astype(BF16), vn_ref[0][:, sl], preferred_element_type=F32)
        z_c = jnp.dot(q, ckt_ref[0, 0, hh].astype(BF16), preferred_element_type=F32)
        sp = _softplus2(z_c).astype(BF16)
        stacked = jnp.concatenate(
            [sp[:, b * KEY_BLOCK:(b + 1) * KEY_BLOCK] for b in range(nblk)], axis=0)
        c_loc = jnp.dot(stacked, tri, preferred_element_type=F32)
        carry = c_n[:, :1]
        cs = [None] * nblk
        for b in reversed(range(nblk)):
            loc = c_loc[b * ts:(b + 1) * ts, :]
            cs[b] = loc + carry
            carry = carry + loc[:, :1]
        w = jnp.exp2(z_c - jnp.concatenate(cs, axis=1))
        pv = pv + lax.dot_general(w.astype(BF16), cvt_ref[0, 0, hh].astype(BF16), NT,
                                  preferred_element_type=F32)
        outs.append(pv)
    o = jnp.concatenate(outs, axis=-1)
    o_ref[0] = _head_norm(o, g_ref[...]).astype(BF16)


def _sb_sample(q, kn, vn, cache_kt, cache_vt, tri, g):
    bsz, ts, _ = q.shape
    past = cache_kt.shape[4]
    new = pl.BlockSpec((1, ts, LANES), lambda b, p: (b, 0, p))
    cache = pl.BlockSpec((1, 1, 2, HEAD_DIM, past), lambda b, p: (0, b, p, 0, 0))
    return pl.pallas_call(
        _sb_sample_kernel,
        grid=(bsz, PAIRS),
        in_specs=[new, new, new, cache, cache,
                  pl.BlockSpec(tri.shape, lambda b, p: (0, 0)),
                  pl.BlockSpec((1, LANES), lambda b, p: (0, p))],
        out_specs=new,
        out_shape=jax.ShapeDtypeStruct((bsz, ts, D_A), BF16),
        compiler_params=pltpu.CompilerParams(
            dimension_semantics=("arbitrary", "arbitrary"), vmem_limit_bytes=VMEM_LIMIT),
        name="sb_sample",
    )(q, kn, vn, cache_kt, cache_vt, tri, g)


def _out_ffn_kernel(x_ref, o_ref, c_ref, wo_ref, g2_ref, wg_ref, wu_ref, wd_ref, gf_ref, y_ref,
                    acc_ref, hf_ref, ga_ref, gb_ref):
    nb, tm, d = x_ref.shape
    m = nb * tm
    n_chunks = wg_ref.shape[0]
    assert n_chunks % 2 == 1
    mix = jnp.dot(o_ref[...].reshape(m, D_A), wo_ref[0], preferred_element_type=F32)
    mix = mix + jnp.dot(c_ref[...].reshape(m, D_C), wo_ref[1], preferred_element_type=F32)
    x1 = x_ref[...].reshape(m, d) + mix
    hf_ref[...] = (x1 * _rms_scale(x1) * g2_ref[...]).astype(BF16)
    acc_ref[...] = x1

    def gate_up(c, gu_ref):
        hf = hf_ref[...]
        gu_ref[0] = jnp.dot(hf, wg_ref[c], preferred_element_type=F32)
        gu_ref[1] = jnp.dot(hf, wu_ref[c], preferred_element_type=F32)

    def down(c, gu_ref):
        gate = gu_ref[0]
        act = (gate * _sigmoid(gate) * gu_ref[1]).astype(BF16)
        acc_ref[...] += jnp.dot(act, wd_ref[c], preferred_element_type=F32)

    gate_up(0, ga_ref)

    def pair(j, _):
        c = 2 * j + 1
        down(c - 1, ga_ref)
        gate_up(c, gb_ref)
        down(c, gb_ref)
        gate_up(c + 1, ga_ref)
        return 0

    lax.fori_loop(0, (n_chunks - 1) // 2, pair, 0)
    down(n_chunks - 1, ga_ref)
    x2 = acc_ref[...]
    y_ref[...] = (x2 * _rms_scale(x2) * gf_ref[...]).reshape(nb, tm, d)


def _out_ffn(x, o, c, wo2, g2, wg, wu, wd, gf, nb, tm):
    bsz, t, d = x.shape
    row = lambda b, i: (b, i, 0)
    act = lambda width: pl.BlockSpec((nb, tm, width), row)
    vec = lambda a: pl.BlockSpec(a.shape, lambda b, i: (0, 0))
    resident = lambda a: pl.BlockSpec(a.shape, lambda b, i: (0, 0, 0), pipeline_mode=pl.Buffered(1))
    return pl.pallas_call(
        _out_ffn_kernel,
        grid=(bsz // nb, t // tm),
        in_specs=[act(d), act(D_A), act(D_C),
                  resident(wo2), vec(g2), resident(wg), resident(wu), resident(wd), vec(gf)],
        out_specs=act(d),
        out_shape=jax.ShapeDtypeStruct((bsz, t, d), F32),
        scratch_shapes=[pltpu.VMEM((nb * tm, d), F32), pltpu.VMEM((nb * tm, d), BF16)]
        + [pltpu.VMEM((2, nb * tm, FF_CHUNK), F32)] * 2,
        compiler_params=pltpu.CompilerParams(
            dimension_semantics=("arbitrary", "arbitrary"), vmem_limit_bytes=VMEM_LIMIT),
        name="out_ffn",
    )(x, o, c, wo2, g2, wg, wu, wd, gf)


def _row(a):
    return a.reshape(1, -1)


def kernel(x_prompt, x_sample, cache_k, cache_v, state_conv, w_in, sb_norm_g, conv_w, conv_b,
           conv_ln_g, conv_ln_b, w_out, norm1_g, norm2_g, w_gate, w_up, w_down, final_g):
    assert w_in.shape[0] == 1, "single-layer step"
    d = x_prompt.shape[-1]
    n_ff = w_gate.shape[-1] // FF_CHUNK

    w5 = w_in[0].astype(BF16).reshape(d, 5, D_A).transpose(1, 0, 2)
    wqag = jnp.stack([w5[0], w5[3], w5[4]])
    wkv = w5[1:3]
    wkvt = wkv.transpose(0, 2, 1)
    wo2 = w_out[0].astype(BF16).reshape(2, D_A, d)
    wg = w_gate[0].astype(BF16).reshape(d, n_ff, FF_CHUNK).transpose(1, 0, 2)
    wu = w_up[0].astype(BF16).reshape(d, n_ff, FF_CHUNK).transpose(1, 0, 2)
    wd = w_down[0].astype(BF16).reshape(n_ff, FF_CHUNK, d)
    g1, g2, gf = _row(norm1_g[0]), _row(norm2_g[0]), _row(final_g)
    sbg = _row(sb_norm_g[0])
    cw, cb = conv_w[0], _row(conv_b[0])
    lg, lb = _row(conv_ln_g[0]), _row(conv_ln_b[0])
    idx = jnp.arange(KEY_BLOCK)
    tri = (idx[:, None] >= idx[None, :]).astype(BF16)
    conv_params = (cw, cb, lg, lb)
    tail = (wo2, g2, wg, wu, wd, gf)

    q, ktb, vtb, ktf, vtf, c, u_tail = _in_proj_prompt(x_prompt, g1, wqag, wkvt, *conv_params)
    o = _sb_prompt(q, ktb, vtb, tri, sbg)
    y_prompt = _out_ffn(x_prompt, o, c, *tail, 1, ROW_TILE)
    k_prompt = jnp.swapaxes(ktf, 3, 4)
    v_prompt = jnp.swapaxes(vtf, 3, 4)
    conv_prompt = u_tail[None, :, HALO - CONV_HIST:, :]

    bs, ts, _ = x_sample.shape
    hist = jnp.pad(state_conv[0], ((0, 0), (HALO - CONV_HIST, 0), (0, 0)))
    qs, kbs, vbs, k_sample, v_sample, cs, us = _in_proj_sample(x_sample, g1, wqag, wkv, hist, *conv_params)
    os_ = _sb_sample(qs, kbs, vbs, jnp.swapaxes(cache_k, 3, 4), jnp.swapaxes(cache_v, 3, 4), tri, sbg)
    y_sample = _out_ffn(x_sample, os_, cs, *tail, bs, ts)
    conv_sample = jnp.concatenate([state_conv[0], us], axis=1)[None, :, -CONV_HIST:, :]

    return (y_prompt, y_sample, k_prompt, v_prompt, conv_prompt, k_sample, v_sample, conv_sample)
```

```python
import jax
import jax.numpy as jnp
from jax import lax
from jax.experimental import pallas as pl
from jax.experimental.pallas import tpu as pltpu

N_HEADS = 8
HEAD_DIM = 64
D_A = N_HEADS * HEAD_DIM
D_C = 512
CONV_W = 31
CONV_HIST = CONV_W - 1
EPS = 1e-6

SUBLANES = 8
LANES = 128
PAIRS = D_A // LANES
HALO = 32
KEY_BLOCK = 256
SAMPLE_HEADS = 4
FF_CHUNK = 256
ROW_TILE = 512
VMEM_LIMIT = 56 * 1024 * 1024

LOG2E = 1.4426950408889634
Q_SCALE = HEAD_DIM ** -0.5 * LOG2E
MASKED = -1e30
UNDERFLOW_BITS = 150.0
SOFTPLUS_SATURATION = 100.0

BF16 = jnp.bfloat16
F32 = jnp.float32

NT = (((1,), (1,)), ((), ()))

Q_COLS = slice(0, D_A)
KV_COLS = (slice(D_A, 2 * D_A), slice(2 * D_A, 3 * D_A))
A_COLS = slice(3 * D_A, 3 * D_A + D_C)
G_COLS = slice(3 * D_A + D_C, 3 * D_A + 2 * D_C)


def _rms_scale(x):
    return lax.rsqrt(jnp.mean(x * x, axis=-1, keepdims=True) + EPS)


def _sigmoid(x):
    return 0.5 + 0.5 * jnp.tanh(0.5 * x)


def _causal_dwconv(ext, cw_ref, cb_ref, tm):
    nb, rows, _ = ext.shape
    conv = jnp.zeros((nb, tm, D_C), F32) + cb_ref[...]
    first = HALO - CONV_HIST
    for r in range(SUBLANES):
        offsets = [m for m in range(first, first + CONV_W) if m % SUBLANES == r]
        shifted = ext if r == 0 else pltpu.roll(ext, rows - r, 1)
        for m in offsets:
            tap = m - first
            conv = conv + shifted[:, m - r:m - r + tm, :] * cw_ref[tap:tap + 1, :]
    return conv


def _conv_act(conv, lg_ref, lb_ref):
    mu = jnp.mean(conv, axis=-1, keepdims=True)
    cc = conv - mu
    ln = cc * lax.rsqrt(jnp.mean(cc * cc, axis=-1, keepdims=True) + EPS) * lg_ref[...] + lb_ref[...]
    return (ln * _sigmoid(ln)).astype(BF16)


def _in_proj_prompt_kernel(x_ref, g1_ref, w_ref, wkvt_ref, cw_ref, cb_ref, lg_ref, lb_ref,
                           q_ref, kb_ref, vb_ref, kf_ref, vf_ref, c_ref, tail_ref, hist_ref):
    tm = x_ref.shape[1]

    @pl.when(pl.program_id(1) == 0)
    def _():
        hist_ref[...] = jnp.zeros_like(hist_ref)

    x = x_ref[0]
    h = (x * _rms_scale(x) * g1_ref[...]).astype(BF16)
    a = jnp.dot(h, w_ref[:, A_COLS], preferred_element_type=F32)
    g = jnp.dot(h, w_ref[:, G_COLS], preferred_element_type=F32)
    u = a * _sigmoid(g)
    ext = jnp.concatenate([hist_ref[...], u], axis=0)
    c_ref[0] = _conv_act(_causal_dwconv(ext[None], cw_ref, cb_ref, tm)[0], lg_ref, lb_ref)
    hist_ref[...] = u[tm - HALO:, :]
    tail_ref[0] = u[tm - HALO:, :]

    q_ref[0] = (jnp.dot(h, w_ref[:, Q_COLS], preferred_element_type=F32) * Q_SCALE).astype(BF16)
    for c, b_ref, f_ref in ((0, kb_ref, kf_ref), (1, vb_ref, vf_ref)):
        pt = lax.dot_general(wkvt_ref[c], h, NT, preferred_element_type=F32)
        f_ref[0, 0] = pt.reshape(N_HEADS, HEAD_DIM, tm)
        for j in range(tm // KEY_BLOCK):
            b_ref[0, j] = pt[:, j * KEY_BLOCK:(j + 1) * KEY_BLOCK].astype(BF16)


def _in_proj_prompt(x, g1, w, wkvt, cw, cb, lg, lb):
    bsz, t, d = x.shape
    tm = ROW_TILE
    per_tile = tm // KEY_BLOCK
    row = lambda b, i: (b, i, 0)
    const2 = lambda b, i: (0, 0)
    const3 = lambda b, i: (0, 0, 0)
    act = lambda width: pl.BlockSpec((1, tm, width), row)
    vec = lambda a: pl.BlockSpec(a.shape, const2)
    kv_f = pl.BlockSpec((1, 1, N_HEADS, HEAD_DIM, tm), lambda b, i: (0, b, 0, 0, i))
    kv_b = pl.BlockSpec((1, per_tile, D_A, KEY_BLOCK), lambda b, i: (b, i, 0, 0))
    resident = lambda a: pl.BlockSpec(a.shape, const3, pipeline_mode=pl.Buffered(1))
    return pl.pallas_call(
        _in_proj_prompt_kernel,
        grid=(bsz, t // tm),
        in_specs=[act(d), vec(g1), pl.BlockSpec(w.shape, const2, pipeline_mode=pl.Buffered(1)),
                  resident(wkvt), vec(cw), vec(cb), vec(lg), vec(lb)],
        out_specs=[act(D_A), kv_b, kv_b, kv_f, kv_f, act(D_C),
                   pl.BlockSpec((1, HALO, D_C), lambda b, i: (b, 0, 0))],
        out_shape=[jax.ShapeDtypeStruct((bsz, t, D_A), BF16)]
        + [jax.ShapeDtypeStruct((bsz, t // KEY_BLOCK, D_A, KEY_BLOCK), BF16)] * 2
        + [jax.ShapeDtypeStruct((1, bsz, N_HEADS, HEAD_DIM, t), F32)] * 2
        + [jax.ShapeDtypeStruct((bsz, t, D_C), BF16), jax.ShapeDtypeStruct((bsz, HALO, D_C), F32)],
        scratch_shapes=[pltpu.VMEM((HALO, D_C), F32)],
        compiler_params=pltpu.CompilerParams(
            dimension_semantics=("arbitrary", "arbitrary"), vmem_limit_bytes=VMEM_LIMIT),
        name="in_proj_prompt",
    )(x, g1, w, wkvt, cw, cb, lg, lb)


def _in_proj_sample_kernel(x_ref, g1_ref, w_ref, hist_ref, cw_ref, cb_ref, lg_ref, lb_ref,
                           q_ref, kb_ref, vb_ref, kf_ref, vf_ref, c_ref, u_ref):
    nb, ts, d = x_ref.shape
    m = nb * ts
    x = x_ref[...].reshape(m, d)
    h = (x * _rms_scale(x) * g1_ref[...]).astype(BF16)
    q_ref[...] = (jnp.dot(h, w_ref[:, Q_COLS], preferred_element_type=F32) * Q_SCALE
                  ).astype(BF16).reshape(nb, ts, D_A)
    for c, b_ref, f_ref in ((0, kb_ref, kf_ref), (1, vb_ref, vf_ref)):
        p = jnp.dot(h, w_ref[:, KV_COLS[c]], preferred_element_type=F32)
        b_ref[...] = p.astype(BF16).reshape(nb, ts, D_A)
        for hd in range(N_HEADS):
            f_ref[0, :, hd, :, :] = p[:, hd * HEAD_DIM:(hd + 1) * HEAD_DIM].reshape(nb, ts, HEAD_DIM)
    a = jnp.dot(h, w_ref[:, A_COLS], preferred_element_type=F32)
    g = jnp.dot(h, w_ref[:, G_COLS], preferred_element_type=F32)
    u = (a * _sigmoid(g)).reshape(nb, ts, D_C)
    u_ref[...] = u
    ext = jnp.concatenate([hist_ref[...], u], axis=1)
    conv = _causal_dwconv(ext, cw_ref, cb_ref, ts).reshape(m, D_C)
    c_ref[...] = _conv_act(conv, lg_ref, lb_ref).reshape(nb, ts, D_C)


def _in_proj_sample(x, g1, w, hist, cw, cb, lg, lb):
    bsz, ts, d = x.shape
    args = (x, g1, w, hist, cw, cb, lg, lb)
    full = lambda shape: pl.BlockSpec(shape, lambda i: (0,) * len(shape))
    act = (bsz, ts, D_A)
    kvf = (1, bsz, N_HEADS, ts, HEAD_DIM)
    return pl.pallas_call(
        _in_proj_sample_kernel,
        grid=(1,),
        in_specs=[full(a.shape) for a in args],
        out_specs=[full(act), full(act), full(act), full(kvf), full(kvf), full(act), full(act)],
        out_shape=[jax.ShapeDtypeStruct(act, BF16)] * 3
        + [jax.ShapeDtypeStruct(kvf, F32)] * 2
        + [jax.ShapeDtypeStruct((bsz, ts, D_C), BF16), jax.ShapeDtypeStruct((bsz, ts, D_C), F32)],
        compiler_params=pltpu.CompilerParams(
            dimension_semantics=("arbitrary",), vmem_limit_bytes=VMEM_LIMIT),
        name="in_proj_sample",
    )(*args)


def _softplus2(z):
    return jnp.maximum(jnp.log(1.0 + jnp.exp2(jnp.minimum(z, SOFTPLUS_SATURATION))) * LOG2E, z)


def _split_heads(x2):
    lane = lax.broadcasted_iota(jnp.int32, x2.shape, 1)
    zero = jnp.zeros_like(x2)
    return jnp.where(lane < HEAD_DIM, x2, zero), jnp.where(lane < HEAD_DIM, zero, x2)


def _head_norm(o, g):
    lane = lax.broadcasted_iota(jnp.int32, o.shape, 1)
    first = lane < HEAD_DIM
    sq = o * o
    ms_a = jnp.sum(jnp.where(first, sq, 0.0), axis=-1, keepdims=True)
    ms_b = jnp.sum(jnp.where(first, 0.0, sq), axis=-1, keepdims=True)
    ms = jnp.where(first, ms_a, ms_b) * (1.0 / HEAD_DIM)
    return o * lax.rsqrt(ms + EPS) * g


def _sb_prompt_kernel(q_ref, kt_ref, vt_ref, tri_ref, g_ref, o_ref, acc_ref, carry_ref):
    i = pl.program_id(1)
    tq = q_ref.shape[1]
    tri = tri_ref[...]
    row = lax.broadcasted_iota(jnp.int32, (tq, KEY_BLOCK), 0)
    col = lax.broadcasted_iota(jnp.int32, (tq, KEY_BLOCK), 1)
    diag = col < row
    qs = [_split_heads(q_ref[0, :, p * LANES:(p + 1) * LANES]) for p in range(PAIRS)]

    def visit(blocks):
        heads = [(p, hh) for p in range(PAIRS) for hh in range(2)]
        zs = {}
        for n, (kb, mask) in enumerate(blocks):
            for hd, (p, hh) in enumerate(heads):
                kt2 = kt_ref[0, kb, p * LANES:(p + 1) * LANES, :]
                z = jnp.dot(qs[p][hh], kt2, preferred_element_type=F32)
                zs[n, hd] = z if mask is None else jnp.where(mask, z, MASKED)
        sps = {key: _softplus2(z).astype(BF16) for key, z in zs.items()}
        locs = {key: jnp.dot(sp, tri, preferred_element_type=F32) for key, sp in sps.items()}
        ws = {}
        lowest = None
        for hd in range(N_HEADS):
            carry = carry_ref[hd]
            for n in range(len(blocks)):
                ws[n, hd] = jnp.exp2(zs[n, hd] - (locs[n, hd] + carry)).astype(BF16)
                carry = carry + locs[n, hd][:, :1]
            carry_ref[hd] = carry
            lowest = carry if lowest is None else jnp.minimum(lowest, carry)
        first = lax.broadcasted_iota(jnp.int32, (tq, LANES), 1) < HEAD_DIM
        for p in range(PAIRS):
            pair = None
            for n, (kb, _) in enumerate(blocks):
                vt2 = vt_ref[0, kb, p * LANES:(p + 1) * LANES, :]
                pvs = [lax.dot_general(ws[n, 2 * p + hh], vt2, NT, preferred_element_type=F32)
                       for hh in range(2)]
                part = jnp.where(first, pvs[0], pvs[1])
                pair = part if pair is None else pair + part
            acc_ref[p] += pair
        return jnp.min(lowest)

    acc_ref[...] = jnp.zeros_like(acc_ref)
    carry_ref[...] = jnp.zeros_like(carry_ref)
    lowest = lax.cond(i == 0,
                      lambda: visit([(i, diag)]),
                      lambda: visit([(i, diag), (i - 1, None)]))

    def more(state):
        j, lowest = state
        return jnp.logical_and(j < i, lowest < UNDERFLOW_BITS)

    def body(state):
        j, _ = state
        return j + 1, visit([(i - 1 - j, None)])

    lax.while_loop(more, body, (jnp.int32(1), lowest))

    for p in range(PAIRS):
        sl = slice(p * LANES, (p + 1) * LANES)
        o_ref[0, :, sl] = _head_norm(acc_ref[p], g_ref[:, sl]).astype(BF16)


def _sb_prompt(q, kt, vt, tri, g):
    bsz, t, _ = q.shape
    tq = KEY_BLOCK
    q_spec = pl.BlockSpec((1, tq, D_A), lambda b, i: (b, i, 0))
    kv_spec = pl.BlockSpec((1,) + kt.shape[1:], lambda b, i: (b, 0, 0, 0))
    return pl.pallas_call(
        _sb_prompt_kernel,
        grid=(bsz, t // tq),
        in_specs=[q_spec, kv_spec, kv_spec,
                  pl.BlockSpec(tri.shape, lambda b, i: (0, 0)),
                  pl.BlockSpec((1, D_A), lambda b, i: (0, 0))],
        out_specs=q_spec,
        out_shape=jax.ShapeDtypeStruct((bsz, t, D_A), BF16),
        scratch_shapes=[pltpu.VMEM((PAIRS, tq, LANES), F32), pltpu.VMEM((N_HEADS, tq, 1), F32)],
        compiler_params=pltpu.CompilerParams(
            dimension_semantics=("arbitrary", "arbitrary"), vmem_limit_bytes=VMEM_LIMIT),
        name="sb_prompt",
    )(q, kt, vt, tri, g)


def _sb_sample_kernel(q_ref, kn_ref, vn_ref, ckt_ref, cvt_ref, tri_ref, g_ref, o_ref):
    ts = q_ref.shape[1]
    past = ckt_ref.shape[4]
    nblk = past // KEY_BLOCK
    tri = tri_ref[...]
    row = lax.broadcasted_iota(jnp.int32, (ts, ts), 0)
    col = lax.broadcasted_iota(jnp.int32, (ts, ts), 1)
    causal = col < row
    heads = range(SAMPLE_HEADS)
    cols = [slice(hd * HEAD_DIM, (hd + 1) * HEAD_DIM) for hd in heads]
    qs = [q_ref[0][:, sl] for sl in cols]

    z_c = [jnp.dot(qs[hd], ckt_ref[0, 0, hd].astype(BF16), preferred_element_type=F32) for hd in heads]
    z_n = [jnp.where(causal, lax.dot_general(qs[hd], kn_ref[0][:, cols[hd]], NT,
                                             preferred_element_type=F32), MASKED) for hd in heads]
    c_n = [jnp.dot(_softplus2(z).astype(BF16), tri[:ts, :ts], preferred_element_type=F32) for z in z_n]
    sps = [_softplus2(z).astype(BF16) for z in z_c]
    locs = [jnp.dot(jnp.concatenate([sp[:, b * KEY_BLOCK:(b + 1) * KEY_BLOCK] for b in range(nblk)],
                                    axis=0), tri, preferred_element_type=F32) for sp in sps]
    outs = []
    for hd in heads:
        carry = c_n[hd][:, :1]
        cs = [None] * nblk
        for b in reversed(range(nblk)):
            loc = locs[hd][b * ts:(b + 1) * ts, :]
            cs[b] = loc + carry
            carry = carry + loc[:, :1]
        w_c = jnp.exp2(z_c[hd] - jnp.concatenate(cs, axis=1)).astype(BF16)
        w_n = jnp.exp2(z_n[hd] - c_n[hd]).astype(BF16)
        outs.append(jnp.dot(w_n, vn_ref[0][:, cols[hd]], preferred_element_type=F32)
                    + lax.dot_general(w_c, cvt_ref[0, 0, hd].astype(BF16), NT,
                                      preferred_element_type=F32))
    for p in range(SAMPLE_HEADS // 2):
        sl = slice(p * LANES, (p + 1) * LANES)
        o = jnp.concatenate(outs[2 * p:2 * p + 2], axis=-1)
        o_ref[0, :, sl] = _head_norm(o, g_ref[:, sl]).astype(BF16)


def _sb_sample(q, kn, vn, cache_kt, cache_vt, tri, g):
    bsz, ts, _ = q.shape
    past = cache_kt.shape[4]
    width = SAMPLE_HEADS * HEAD_DIM
    new = pl.BlockSpec((1, ts, width), lambda b, p: (b, 0, p))
    cache = pl.BlockSpec((1, 1, SAMPLE_HEADS, HEAD_DIM, past), lambda b, p: (0, b, p, 0, 0))
    return pl.pallas_call(
        _sb_sample_kernel,
        grid=(bsz, N_HEADS // SAMPLE_HEADS),
        in_specs=[new, new, new, cache, cache,
                  pl.BlockSpec(tri.shape, lambda b, p: (0, 0)),
                  pl.BlockSpec((1, width), lambda b, p: (0, p))],
        out_specs=new,
        out_shape=jax.ShapeDtypeStruct((bsz, ts, D_A), BF16),
        compiler_params=pltpu.CompilerParams(
            dimension_semantics=("arbitrary", "arbitrary"), vmem_limit_bytes=VMEM_LIMIT),
        name="sb_sample",
    )(q, kn, vn, cache_kt, cache_vt, tri, g)


def _out_ffn_kernel(x_ref, o_ref, c_ref, wo_ref, g2_ref, wg_ref, wu_ref, wd_ref, gf_ref, y_ref,
                    acc_ref, hf_ref, ga_ref, gb_ref):
    nb, tm, d = x_ref.shape
    m = nb * tm
    n_chunks = wg_ref.shape[1] // FF_CHUNK
    assert n_chunks % 2 == 1 and n_chunks * FF_CHUNK == wg_ref.shape[1]
    mix = jnp.dot(o_ref[...].reshape(m, D_A), wo_ref[0], preferred_element_type=F32)
    mix = mix + jnp.dot(c_ref[...].reshape(m, D_C), wo_ref[1], preferred_element_type=F32)
    x1 = x_ref[...].reshape(m, d) + mix
    hf_ref[...] = (x1 * _rms_scale(x1) * g2_ref[...]).astype(BF16)
    acc_ref[...] = x1

    def chunk(c):
        return pl.ds(pl.multiple_of(c * FF_CHUNK, FF_CHUNK), FF_CHUNK)

    def gate_up(c, gu_ref):
        hf = hf_ref[...]
        gu_ref[0] = jnp.dot(hf, wg_ref[:, chunk(c)], preferred_element_type=F32)
        gu_ref[1] = jnp.dot(hf, wu_ref[:, chunk(c)], preferred_element_type=F32)

    def down(c, gu_ref):
        gate = gu_ref[0]
        act = (gate * _sigmoid(gate) * gu_ref[1]).astype(BF16)
        acc_ref[...] += jnp.dot(act, wd_ref[chunk(c), :], preferred_element_type=F32)

    gate_up(0, ga_ref)

    def pair(j, _):
        c = 2 * j + 1
        down(c - 1, ga_ref)
        gate_up(c, gb_ref)
        down(c, gb_ref)
        gate_up(c + 1, ga_ref)
        return 0

    lax.fori_loop(0, (n_chunks - 1) // 2, pair, 0)
    down(n_chunks - 1, ga_ref)
    x2 = acc_ref[...]
    y_ref[...] = (x2 * _rms_scale(x2) * gf_ref[...]).reshape(nb, tm, d)


def _out_ffn(x, o, c, wo2, g2, wg, wu, wd, gf, nb, tm):
    bsz, t, d = x.shape
    row = lambda b, i: (b, i, 0)
    act = lambda width: pl.BlockSpec((nb, tm, width), row)
    vec = lambda a: pl.BlockSpec(a.shape, lambda b, i: (0, 0))
    resident = lambda a: pl.BlockSpec(a.shape, lambda b, i: (0,) * a.ndim, pipeline_mode=pl.Buffered(1))
    return pl.pallas_call(
        _out_ffn_kernel,
        grid=(bsz // nb, t // tm),
        in_specs=[act(d), act(D_A), act(D_C),
                  resident(wo2), vec(g2), resident(wg), resident(wu), resident(wd), vec(gf)],
        out_specs=act(d),
        out_shape=jax.ShapeDtypeStruct((bsz, t, d), F32),
        scratch_shapes=[pltpu.VMEM((nb * tm, d), F32), pltpu.VMEM((nb * tm, d), BF16)]
        + [pltpu.VMEM((2, nb * tm, FF_CHUNK), F32)] * 2,
        compiler_params=pltpu.CompilerParams(
            dimension_semantics=("arbitrary", "arbitrary"), vmem_limit_bytes=VMEM_LIMIT),
        name="out_ffn",
    )(x, o, c, wo2, g2, wg, wu, wd, gf)


def _row(a):
    return a.reshape(1, -1)


def kernel(x_prompt, x_sample, cache_k, cache_v, state_conv, w_in, sb_norm_g, conv_w, conv_b,
           conv_ln_g, conv_ln_b, w_out, norm1_g, norm2_g, w_gate, w_up, w_down, final_g):
    assert w_in.shape[0] == 1, "single-layer step"
    d = x_prompt.shape[-1]

    w = w_in[0].astype(BF16)
    wkvt = w[:, D_A:3 * D_A].T.reshape(2, D_A, d)
    wo2 = w_out[0].astype(BF16).reshape(2, D_A, d)
    wg, wu, wd = w_gate[0].astype(BF16), w_up[0].astype(BF16), w_down[0].astype(BF16)
    g1, g2, gf = _row(norm1_g[0]), _row(norm2_g[0]), _row(final_g)
    sbg = _row(sb_norm_g[0])
    cw, cb = conv_w[0], _row(conv_b[0])
    lg, lb = _row(conv_ln_g[0]), _row(conv_ln_b[0])
    idx = jnp.arange(KEY_BLOCK)
    tri = (idx[:, None] >= idx[None, :]).astype(BF16)
    conv_params = (cw, cb, lg, lb)
    tail = (wo2, g2, wg, wu, wd, gf)

    q, ktb, vtb, ktf, vtf, c, u_tail = _in_proj_prompt(x_prompt, g1, w, wkvt, *conv_params)
    o = _sb_prompt(q, ktb, vtb, tri, sbg)
    y_prompt = _out_ffn(x_prompt, o, c, *tail, 1, ROW_TILE)
    k_prompt = jnp.swapaxes(ktf, 3, 4)
    v_prompt = jnp.swapaxes(vtf, 3, 4)
    conv_prompt = u_tail[None, :, HALO - CONV_HIST:, :]

    bs, ts, _ = x_sample.shape
    hist = jnp.pad(state_conv[0], ((0, 0), (HALO - CONV_HIST, 0), (0, 0)))
    qs, kbs, vbs, k_sample, v_sample, cs, us = _in_proj_sample(x_sample, g1, w, hist, *conv_params)
    os_ = _sb_sample(qs, kbs, vbs, jnp.swapaxes(cache_k, 3, 4), jnp.swapaxes(cache_v, 3, 4), tri, sbg)
    y_sample = _out_ffn(x_sample, os_, cs, *tail, bs, ts)
    conv_sample = jnp.concatenate([state_conv[0], us], axis=1)[None, :, -CONV_HIST:, :]

    return (y_prompt, y_sample, k_prompt, v_prompt, conv_prompt, k_sample, v_sample, conv_sample)
```

```python
import jax
import jax.numpy as jnp
from jax import lax
from jax.experimental import pallas as pl
from jax.experimental.pallas import tpu as pltpu

N_HEADS = 8
HEAD_DIM = 64
D_A = N_HEADS * HEAD_DIM
D_C = 512
CONV_W = 31
CONV_HIST = CONV_W - 1
EPS = 1e-6

SUBLANES = 8
LANES = 128
PAIRS = D_A // LANES
HALO = 32
KEY_BLOCK = 256
SAMPLE_HEADS = 4
FF_CHUNK = 256
ROW_TILE = 512
VMEM_LIMIT = 56 * 1024 * 1024

LOG2E = 1.4426950408889634
Q_SCALE = HEAD_DIM ** -0.5 * LOG2E
MASKED = -1e30
UNDERFLOW_BITS = 150.0
SOFTPLUS_SATURATION = 100.0

BF16 = jnp.bfloat16
F32 = jnp.float32

NT = (((1,), (1,)), ((), ()))

Q_COLS = slice(0, D_A)
KV_COLS = (slice(D_A, 2 * D_A), slice(2 * D_A, 3 * D_A))
A_COLS = slice(3 * D_A, 3 * D_A + D_C)
G_COLS = slice(3 * D_A + D_C, 3 * D_A + 2 * D_C)


def _rms_scale(x):
    return lax.rsqrt(jnp.mean(x * x, axis=-1, keepdims=True) + EPS)


def _sigmoid(x):
    return 0.5 + 0.5 * jnp.tanh(0.5 * x)


def _causal_dwconv(ext, cw_ref, cb_ref, tm):
    nb, rows, _ = ext.shape
    conv = jnp.zeros((nb, tm, D_C), F32) + cb_ref[...]
    first = HALO - CONV_HIST
    for r in range(SUBLANES):
        offsets = [m for m in range(first, first + CONV_W) if m % SUBLANES == r]
        shifted = ext if r == 0 else pltpu.roll(ext, rows - r, 1)
        for m in offsets:
            tap = m - first
            conv = conv + shifted[:, m - r:m - r + tm, :] * cw_ref[tap:tap + 1, :]
    return conv


def _conv_act(conv, lg_ref, lb_ref):
    mu = jnp.mean(conv, axis=-1, keepdims=True)
    cc = conv - mu
    ln = cc * lax.rsqrt(jnp.mean(cc * cc, axis=-1, keepdims=True) + EPS) * lg_ref[...] + lb_ref[...]
    return (ln * _sigmoid(ln)).astype(BF16)


def _in_proj_prompt_kernel(x_ref, g1_ref, w_ref, wkvt_ref, cw_ref, cb_ref, lg_ref, lb_ref,
                           q_ref, kb_ref, vb_ref, kf_ref, vf_ref, c_ref, tail_ref, hist_ref):
    tm = x_ref.shape[1]

    @pl.when(pl.program_id(1) == 0)
    def _():
        hist_ref[...] = jnp.zeros_like(hist_ref)

    x = x_ref[0]
    h = (x * _rms_scale(x) * g1_ref[...]).astype(BF16)
    a = jnp.dot(h, w_ref[:, A_COLS], preferred_element_type=F32)
    g = jnp.dot(h, w_ref[:, G_COLS], preferred_element_type=F32)
    u = a * _sigmoid(g)
    ext = jnp.concatenate([hist_ref[...], u], axis=0)
    c_ref[0] = _conv_act(_causal_dwconv(ext[None], cw_ref, cb_ref, tm)[0], lg_ref, lb_ref)
    hist_ref[...] = u[tm - HALO:, :]
    tail_ref[0] = u[tm - HALO:, :]

    q_ref[0] = (jnp.dot(h, w_ref[:, Q_COLS], preferred_element_type=F32) * Q_SCALE).astype(BF16)
    for c, b_ref, f_ref in ((0, kb_ref, kf_ref), (1, vb_ref, vf_ref)):
        pt = lax.dot_general(wkvt_ref[c], h, NT, preferred_element_type=F32)
        f_ref[0, 0] = pt.reshape(N_HEADS, HEAD_DIM, tm)
        for j in range(tm // KEY_BLOCK):
            b_ref[0, j] = pt[:, j * KEY_BLOCK:(j + 1) * KEY_BLOCK].astype(BF16)


def _in_proj_prompt(x, g1, w, wkvt, cw, cb, lg, lb):
    bsz, t, d = x.shape
    tm = ROW_TILE
    per_tile = tm // KEY_BLOCK
    row = lambda b, i: (b, i, 0)
    const2 = lambda b, i: (0, 0)
    const3 = lambda b, i: (0, 0, 0)
    act = lambda width: pl.BlockSpec((1, tm, width), row)
    vec = lambda a: pl.BlockSpec(a.shape, const2)
    kv_f = pl.BlockSpec((1, 1, N_HEADS, HEAD_DIM, tm), lambda b, i: (0, b, 0, 0, i))
    kv_b = pl.BlockSpec((1, per_tile, D_A, KEY_BLOCK), lambda b, i: (b, i, 0, 0))
    resident = lambda a: pl.BlockSpec(a.shape, const3, pipeline_mode=pl.Buffered(1))
    return pl.pallas_call(
        _in_proj_prompt_kernel,
        grid=(bsz, t // tm),
        in_specs=[act(d), vec(g1), pl.BlockSpec(w.shape, const2, pipeline_mode=pl.Buffered(1)),
                  resident(wkvt), vec(cw), vec(cb), vec(lg), vec(lb)],
        out_specs=[act(D_A), kv_b, kv_b, kv_f, kv_f, act(D_C),
                   pl.BlockSpec((1, HALO, D_C), lambda b, i: (b, 0, 0))],
        out_shape=[jax.ShapeDtypeStruct((bsz, t, D_A), BF16)]
        + [jax.ShapeDtypeStruct((bsz, t // KEY_BLOCK, D_A, KEY_BLOCK), BF16)] * 2
        + [jax.ShapeDtypeStruct((1, bsz, N_HEADS, HEAD_DIM, t), F32)] * 2
        + [jax.ShapeDtypeStruct((bsz, t, D_C), BF16), jax.ShapeDtypeStruct((bsz, HALO, D_C), F32)],
        scratch_shapes=[pltpu.VMEM((HALO, D_C), F32)],
        compiler_params=pltpu.CompilerParams(
            dimension_semantics=("arbitrary", "arbitrary"), vmem_limit_bytes=VMEM_LIMIT),
        name="in_proj_prompt",
    )(x, g1, w, wkvt, cw, cb, lg, lb)


def _in_proj_sample_kernel(x_ref, g1_ref, w_ref, hist_ref, cw_ref, cb_ref, lg_ref, lb_ref,
                           q_ref, kb_ref, vb_ref, kf_ref, vf_ref, c_ref, u_ref):
    nb, ts, d = x_ref.shape
    m = nb * ts
    x = x_ref[...].reshape(m, d)
    h = (x * _rms_scale(x) * g1_ref[...]).astype(BF16)
    q_ref[...] = (jnp.dot(h, w_ref[:, Q_COLS], preferred_element_type=F32) * Q_SCALE
                  ).astype(BF16).reshape(nb, ts, D_A)
    for c, b_ref, f_ref in ((0, kb_ref, kf_ref), (1, vb_ref, vf_ref)):
        p = jnp.dot(h, w_ref[:, KV_COLS[c]], preferred_element_type=F32)
        b_ref[...] = p.astype(BF16).reshape(nb, ts, D_A)
        for hd in range(N_HEADS):
            f_ref[0, :, hd, :, :] = p[:, hd * HEAD_DIM:(hd + 1) * HEAD_DIM].reshape(nb, ts, HEAD_DIM)
    a = jnp.dot(h, w_ref[:, A_COLS], preferred_element_type=F32)
    g = jnp.dot(h, w_ref[:, G_COLS], preferred_element_type=F32)
    u = (a * _sigmoid(g)).reshape(nb, ts, D_C)
    u_ref[...] = u
    ext = jnp.concatenate([hist_ref[...], u], axis=1)
    conv = _causal_dwconv(ext, cw_ref, cb_ref, ts).reshape(m, D_C)
    c_ref[...] = _conv_act(conv, lg_ref, lb_ref).reshape(nb, ts, D_C)


def _in_proj_sample(x, g1, w, hist, cw, cb, lg, lb):
    bsz, ts, d = x.shape
    args = (x, g1, w, hist, cw, cb, lg, lb)
    full = lambda shape: pl.BlockSpec(shape, lambda i: (0,) * len(shape))
    act = (bsz, ts, D_A)
    kvf = (1, bsz, N_HEADS, ts, HEAD_DIM)
    return pl.pallas_call(
        _in_proj_sample_kernel,
        grid=(1,),
        in_specs=[full(a.shape) for a in args],
        out_specs=[full(act), full(act), full(act), full(kvf), full(kvf), full(act), full(act)],
        out_shape=[jax.ShapeDtypeStruct(act, BF16)] * 3
        + [jax.ShapeDtypeStruct(kvf, F32)] * 2
        + [jax.ShapeDtypeStruct((bsz, ts, D_C), BF16), jax.ShapeDtypeStruct((bsz, ts, D_C), F32)],
        compiler_params=pltpu.CompilerParams(
            dimension_semantics=("arbitrary",), vmem_limit_bytes=VMEM_LIMIT),
        name="in_proj_sample",
    )(*args)


def _softplus2(z):
    return jnp.maximum(jnp.log(1.0 + jnp.exp2(jnp.minimum(z, SOFTPLUS_SATURATION))) * LOG2E, z)


def _split_heads(x2):
    lane = lax.broadcasted_iota(jnp.int32, x2.shape, 1)
    zero = jnp.zeros_like(x2)
    return jnp.where(lane < HEAD_DIM, x2, zero), jnp.where(lane < HEAD_DIM, zero, x2)


def _head_norm(o, g, ones):
    sq = o * o
    hi = sq.astype(BF16)
    lo = (sq - hi.astype(F32)).astype(BF16)
    ms = jnp.dot(hi, ones, preferred_element_type=F32) + jnp.dot(lo, ones, preferred_element_type=F32)
    return o * lax.rsqrt(ms + EPS) * g


def _sb_prompt_kernel(q_ref, kt_ref, vt_ref, tri_ref, ones_ref, g_ref, o_ref, acc_ref, carry_ref):
    i = pl.program_id(1)
    tq = q_ref.shape[1]
    tri = tri_ref[...]
    row = lax.broadcasted_iota(jnp.int32, (tq, KEY_BLOCK), 0)
    col = lax.broadcasted_iota(jnp.int32, (tq, KEY_BLOCK), 1)
    diag = col < row
    qs = [_split_heads(q_ref[0, :, p * LANES:(p + 1) * LANES]) for p in range(PAIRS)]

    def visit(blocks, fresh):
        heads = [(p, hh) for p in range(PAIRS) for hh in range(2)]
        zs = {}
        for n, (kb, mask) in enumerate(blocks):
            for hd, (p, hh) in enumerate(heads):
                kt2 = kt_ref[0, kb, p * LANES:(p + 1) * LANES, :]
                z = jnp.dot(qs[p][hh], kt2, preferred_element_type=F32)
                zs[n, hd] = z if mask is None else jnp.where(mask, z, MASKED)
        sps = {key: _softplus2(z).astype(BF16) for key, z in zs.items()}
        locs = {key: jnp.dot(sp, tri, preferred_element_type=F32) for key, sp in sps.items()}
        ws = {}
        lowest = None
        for hd in range(N_HEADS):
            carry = None if fresh else carry_ref[hd]
            for n in range(len(blocks)):
                total = locs[n, hd] if carry is None else locs[n, hd] + carry
                ws[n, hd] = jnp.exp2(zs[n, hd] - total).astype(BF16)
                carry = locs[n, hd][:, :1] if carry is None else carry + locs[n, hd][:, :1]
            carry_ref[hd] = carry
            lowest = carry if lowest is None else jnp.minimum(lowest, carry)
        first = lax.broadcasted_iota(jnp.int32, (tq, LANES), 1) < HEAD_DIM
        for p in range(PAIRS):
            pair = None
            for n, (kb, _) in enumerate(blocks):
                vt2 = vt_ref[0, kb, p * LANES:(p + 1) * LANES, :]
                pvs = [lax.dot_general(ws[n, 2 * p + hh], vt2, NT, preferred_element_type=F32)
                       for hh in range(2)]
                part = jnp.where(first, pvs[0], pvs[1])
                pair = part if pair is None else pair + part
            if fresh:
                acc_ref[p] = pair
            else:
                acc_ref[p] += pair
        return jnp.min(lowest)

    lowest = lax.cond(i == 0,
                      lambda: visit([(i, diag)], True),
                      lambda: visit([(i, diag), (i - 1, None)], True))

    def more(state):
        j, lowest = state
        return jnp.logical_and(j < i, lowest < UNDERFLOW_BITS)

    def body(state):
        j, _ = state
        return j + 1, visit([(i - 1 - j, None)], False)

    lax.while_loop(more, body, (jnp.int32(1), lowest))

    for p in range(PAIRS):
        sl = slice(p * LANES, (p + 1) * LANES)
        o_ref[0, :, sl] = _head_norm(acc_ref[p], g_ref[:, sl], ones_ref[...]).astype(BF16)


def _sb_prompt(q, kt, vt, tri, ones, g):
    bsz, t, _ = q.shape
    tq = KEY_BLOCK
    q_spec = pl.BlockSpec((1, tq, D_A), lambda b, i: (b, i, 0))
    kv_spec = pl.BlockSpec((1,) + kt.shape[1:], lambda b, i: (b, 0, 0, 0))
    return pl.pallas_call(
        _sb_prompt_kernel,
        grid=(bsz, t // tq),
        in_specs=[q_spec, kv_spec, kv_spec,
                  pl.BlockSpec(tri.shape, lambda b, i: (0, 0)),
                  pl.BlockSpec(ones.shape, lambda b, i: (0, 0)),
                  pl.BlockSpec((1, D_A), lambda b, i: (0, 0))],
        out_specs=q_spec,
        out_shape=jax.ShapeDtypeStruct((bsz, t, D_A), BF16),
        scratch_shapes=[pltpu.VMEM((PAIRS, tq, LANES), F32), pltpu.VMEM((N_HEADS, tq, 1), F32)],
        compiler_params=pltpu.CompilerParams(
            dimension_semantics=("arbitrary", "arbitrary"), vmem_limit_bytes=VMEM_LIMIT),
        name="sb_prompt",
    )(q, kt, vt, tri, ones, g)


def _sb_sample_kernel(q_ref, kn_ref, vn_ref, ckt_ref, cvt_ref, tri_ref, ones_ref, g_ref, o_ref):
    ts = q_ref.shape[1]
    past = ckt_ref.shape[4]
    nblk = past // KEY_BLOCK
    tri = tri_ref[...]
    row = lax.broadcasted_iota(jnp.int32, (ts, ts), 0)
    col = lax.broadcasted_iota(jnp.int32, (ts, ts), 1)
    causal = col < row
    heads = range(SAMPLE_HEADS)
    cols = [slice(hd * HEAD_DIM, (hd + 1) * HEAD_DIM) for hd in heads]
    qs = [q_ref[0][:, sl] for sl in cols]

    z_c = [jnp.dot(qs[hd], ckt_ref[0, 0, hd].astype(BF16), preferred_element_type=F32) for hd in heads]
    z_n = [jnp.where(causal, lax.dot_general(qs[hd], kn_ref[0][:, cols[hd]], NT,
                                             preferred_element_type=F32), MASKED) for hd in heads]
    c_n = [jnp.dot(_softplus2(z).astype(BF16), tri[:ts, :ts], preferred_element_type=F32) for z in z_n]
    sps = [_softplus2(z).astype(BF16) for z in z_c]
    locs = [jnp.dot(jnp.concatenate([sp[:, b * KEY_BLOCK:(b + 1) * KEY_BLOCK] for b in range(nblk)],
                                    axis=0), tri, preferred_element_type=F32) for sp in sps]
    outs = []
    for hd in heads:
        carry = c_n[hd][:, :1]
        cs = [None] * nblk
        for b in reversed(range(nblk)):
            loc = locs[hd][b * ts:(b + 1) * ts, :]
            cs[b] = loc + carry
            carry = carry + loc[:, :1]
        w_c = jnp.exp2(z_c[hd] - jnp.concatenate(cs, axis=1)).astype(BF16)
        w_n = jnp.exp2(z_n[hd] - c_n[hd]).astype(BF16)
        outs.append(jnp.dot(w_n, vn_ref[0][:, cols[hd]], preferred_element_type=F32)
                    + lax.dot_general(w_c, cvt_ref[0, 0, hd].astype(BF16), NT,
                                      preferred_element_type=F32))
    for p in range(SAMPLE_HEADS // 2):
        sl = slice(p * LANES, (p + 1) * LANES)
        o = jnp.concatenate(outs[2 * p:2 * p + 2], axis=-1)
        o_ref[0, :, sl] = _head_norm(o, g_ref[:, sl], ones_ref[...]).astype(BF16)


def _sb_sample(q, kn, vn, cache_kt, cache_vt, tri, ones, g):
    bsz, ts, _ = q.shape
    past = cache_kt.shape[4]
    width = SAMPLE_HEADS * HEAD_DIM
    new = pl.BlockSpec((1, ts, width), lambda b, p: (b, 0, p))
    cache = pl.BlockSpec((1, 1, SAMPLE_HEADS, HEAD_DIM, past), lambda b, p: (0, b, p, 0, 0))
    return pl.pallas_call(
        _sb_sample_kernel,
        grid=(bsz, N_HEADS // SAMPLE_HEADS),
        in_specs=[new, new, new, cache, cache,
                  pl.BlockSpec(tri.shape, lambda b, p: (0, 0)),
                  pl.BlockSpec(ones.shape, lambda b, p: (0, 0)),
                  pl.BlockSpec((1, width), lambda b, p: (0, p))],
        out_specs=new,
        out_shape=jax.ShapeDtypeStruct((bsz, ts, D_A), BF16),
        compiler_params=pltpu.CompilerParams(
            dimension_semantics=("arbitrary", "arbitrary"), vmem_limit_bytes=VMEM_LIMIT),
        name="sb_sample",
    )(q, kn, vn, cache_kt, cache_vt, tri, ones, g)


def _out_ffn_kernel(x_ref, o_ref, c_ref, wo_ref, g2_ref, wg_ref, wu_ref, wd_ref, gf_ref, y_ref,
                    acc_ref, hf_ref, ga_ref, gb_ref):
    nb, tm, d = x_ref.shape
    m = nb * tm
    n_chunks = wg_ref.shape[1] // FF_CHUNK
    assert n_chunks % 2 == 1 and n_chunks * FF_CHUNK == wg_ref.shape[1]
    mix = jnp.dot(o_ref[...].reshape(m, D_A), wo_ref[0], preferred_element_type=F32)
    mix = mix + jnp.dot(c_ref[...].reshape(m, D_C), wo_ref[1], preferred_element_type=F32)
    x1 = x_ref[...].reshape(m, d) + mix
    hf_ref[...] = (x1 * _rms_scale(x1) * g2_ref[...]).astype(BF16)
    acc_ref[...] = x1

    def chunk(c):
        return pl.ds(pl.multiple_of(c * FF_CHUNK, FF_CHUNK), FF_CHUNK)

    def gate_up(c, gu_ref):
        hf = hf_ref[...]
        gu_ref[0] = jnp.dot(hf, wg_ref[:, chunk(c)], preferred_element_type=F32)
        gu_ref[1] = jnp.dot(hf, wu_ref[:, chunk(c)], preferred_element_type=F32)

    def down(c, gu_ref):
        gate = gu_ref[0]
        act = (gate * _sigmoid(gate) * gu_ref[1]).astype(BF16)
        acc_ref[...] += jnp.dot(act, wd_ref[chunk(c), :], preferred_element_type=F32)

    gate_up(0, ga_ref)

    def pair(j, _):
        c = 2 * j + 1
        down(c - 1, ga_ref)
        gate_up(c, gb_ref)
        down(c, gb_ref)
        gate_up(c + 1, ga_ref)
        return 0

    lax.fori_loop(0, (n_chunks - 1) // 2, pair, 0)
    down(n_chunks - 1, ga_ref)
    x2 = acc_ref[...]
    y_ref[...] = (x2 * _rms_scale(x2) * gf_ref[...]).reshape(nb, tm, d)


def _out_ffn(x, o, c, wo2, g2, wg, wu, wd, gf, nb, tm):
    bsz, t, d = x.shape
    row = lambda b, i: (b, i, 0)
    act = lambda width: pl.BlockSpec((nb, tm, width), row)
    vec = lambda a: pl.BlockSpec(a.shape, lambda b, i: (0, 0))
    resident = lambda a: pl.BlockSpec(a.shape, lambda b, i: (0,) * a.ndim, pipeline_mode=pl.Buffered(1))
    return pl.pallas_call(
        _out_ffn_kernel,
        grid=(bsz // nb, t // tm),
        in_specs=[act(d), act(D_A), act(D_C),
                  resident(wo2), vec(g2), resident(wg), resident(wu), resident(wd), vec(gf)],
        out_specs=act(d),
        out_shape=jax.ShapeDtypeStruct((bsz, t, d), F32),
        scratch_shapes=[pltpu.VMEM((nb * tm, d), F32), pltpu.VMEM((nb * tm, d), BF16)]
        + [pltpu.VMEM((2, nb * tm, FF_CHUNK), F32)] * 2,
        compiler_params=pltpu.CompilerParams(
            dimension_semantics=("arbitrary", "arbitrary"), vmem_limit_bytes=VMEM_LIMIT),
        name="out_ffn",
    )(x, o, c, wo2, g2, wg, wu, wd, gf)


def _row(a):
    return a.reshape(1, -1)


def kernel(x_prompt, x_sample, cache_k, cache_v, state_conv, w_in, sb_norm_g, conv_w, conv_b,
           conv_ln_g, conv_ln_b, w_out, norm1_g, norm2_g, w_gate, w_up, w_down, final_g):
    assert w_in.shape[0] == 1, "single-layer step"
    d = x_prompt.shape[-1]

    w = w_in[0].astype(BF16)
    wkvt = w[:, D_A:3 * D_A].T.reshape(2, D_A, d)
    wo2 = w_out[0].astype(BF16).reshape(2, D_A, d)
    wg, wu, wd = w_gate[0].astype(BF16), w_up[0].astype(BF16), w_down[0].astype(BF16)
    g1, g2, gf = _row(norm1_g[0]), _row(norm2_g[0]), _row(final_g)
    sbg = _row(sb_norm_g[0])
    cw, cb = conv_w[0], _row(conv_b[0])
    lg, lb = _row(conv_ln_g[0]), _row(conv_ln_b[0])
    idx = jnp.arange(KEY_BLOCK)
    tri = (idx[:, None] >= idx[None, :]).astype(BF16)
    lane = jnp.arange(LANES) // HEAD_DIM
    ones = ((lane[:, None] == lane[None, :]) * (1.0 / HEAD_DIM)).astype(BF16)
    conv_params = (cw, cb, lg, lb)
    tail = (wo2, g2, wg, wu, wd, gf)

    q, ktb, vtb, ktf, vtf, c, u_tail = _in_proj_prompt(x_prompt, g1, w, wkvt, *conv_params)
    o = _sb_prompt(q, ktb, vtb, tri, ones, sbg)
    y_prompt = _out_ffn(x_prompt, o, c, *tail, 1, ROW_TILE)
    k_prompt = jnp.swapaxes(ktf, 3, 4)
    v_prompt = jnp.swapaxes(vtf, 3, 4)
    conv_prompt = u_tail[None, :, HALO - CONV_HIST:, :]

    bs, ts, _ = x_sample.shape
    hist = jnp.pad(state_conv[0], ((0, 0), (HALO - CONV_HIST, 0), (0, 0)))
    qs, kbs, vbs, k_sample, v_sample, cs, us = _in_proj_sample(x_sample, g1, w, hist, *conv_params)
    os_ = _sb_sample(qs, kbs, vbs, jnp.swapaxes(cache_k, 3, 4), jnp.swapaxes(cache_v, 3, 4), tri, ones, sbg)
    y_sample = _out_ffn(x_sample, os_, cs, *tail, bs, ts)
    conv_sample = jnp.concatenate([state_conv[0], us], axis=1)[None, :, -CONV_HIST:, :]

    return (y_prompt, y_sample, k_prompt, v_prompt, conv_prompt, k_sample, v_sample, conv_sample)
```

```python
import jax
import jax.numpy as jnp
from jax import lax
from jax.experimental import pallas as pl
from jax.experimental.pallas import tpu as pltpu

N_HEADS = 8
HEAD_DIM = 64
D_A = N_HEADS * HEAD_DIM
D_C = 512
CONV_W = 31
CONV_HIST = CONV_W - 1
EPS = 1e-6

SUBLANES = 8
LANES = 128
PAIRS = D_A // LANES
HALO = 32
KEY_BLOCK = 256
SAMPLE_HEADS = 4
FF_CHUNK = 256
ROW_TILE = 512
VMEM_LIMIT = 56 * 1024 * 1024

LOG2E = 1.4426950408889634
Q_SCALE = HEAD_DIM ** -0.5 * LOG2E
MASKED = -1e30
UNDERFLOW_BITS = 150.0
SOFTPLUS_SATURATION = 100.0

BF16 = jnp.bfloat16
F32 = jnp.float32

NT = (((1,), (1,)), ((), ()))

Q_COLS = slice(0, D_A)
KV_COLS = (slice(D_A, 2 * D_A), slice(2 * D_A, 3 * D_A))
A_COLS = slice(3 * D_A, 3 * D_A + D_C)
G_COLS = slice(3 * D_A + D_C, 3 * D_A + 2 * D_C)


def _rms_scale(x):
    return lax.rsqrt(jnp.mean(x * x, axis=-1, keepdims=True) + EPS)


def _sigmoid(x):
    return 0.5 + 0.5 * jnp.tanh(0.5 * x)


def _causal_dwconv(ext, cw_ref, cb_ref, tm):
    nb, rows, _ = ext.shape
    conv = jnp.zeros((nb, tm, D_C), F32) + cb_ref[...]
    first = HALO - CONV_HIST
    for r in range(SUBLANES):
        offsets = [m for m in range(first, first + CONV_W) if m % SUBLANES == r]
        shifted = ext if r == 0 else pltpu.roll(ext, rows - r, 1)
        for m in offsets:
            tap = m - first
            conv = conv + shifted[:, m - r:m - r + tm, :] * cw_ref[tap:tap + 1, :]
    return conv


def _conv_act(conv, lg_ref, lb_ref):
    mu = jnp.mean(conv, axis=-1, keepdims=True)
    cc = conv - mu
    ln = cc * lax.rsqrt(jnp.mean(cc * cc, axis=-1, keepdims=True) + EPS) * lg_ref[...] + lb_ref[...]
    return (ln * _sigmoid(ln)).astype(BF16)


def _in_proj_prompt_kernel(x_ref, g1_ref, w_ref, wkvt_ref, cw_ref, cb_ref, lg_ref, lb_ref,
                           q_ref, kb_ref, vb_ref, kf_ref, vf_ref, c_ref, tail_ref, hist_ref):
    tm = x_ref.shape[1]

    @pl.when(pl.program_id(1) == 0)
    def _():
        hist_ref[...] = jnp.zeros_like(hist_ref)

    x = x_ref[0]
    h = (x * _rms_scale(x) * g1_ref[...]).astype(BF16)
    a = jnp.dot(h, w_ref[:, A_COLS], preferred_element_type=F32)
    g = jnp.dot(h, w_ref[:, G_COLS], preferred_element_type=F32)
    u = a * _sigmoid(g)
    ext = jnp.concatenate([hist_ref[...], u], axis=0)
    c_ref[0] = _conv_act(_causal_dwconv(ext[None], cw_ref, cb_ref, tm)[0], lg_ref, lb_ref)
    hist_ref[...] = u[tm - HALO:, :]
    tail_ref[0] = u[tm - HALO:, :]

    q_ref[0] = (jnp.dot(h, w_ref[:, Q_COLS], preferred_element_type=F32) * Q_SCALE).astype(BF16)
    for c, b_ref, f_ref in ((0, kb_ref, kf_ref), (1, vb_ref, vf_ref)):
        pt = lax.dot_general(wkvt_ref[c], h, NT, preferred_element_type=F32)
        f_ref[0, 0] = pt.reshape(N_HEADS, HEAD_DIM, tm)
        for j in range(tm // KEY_BLOCK):
            b_ref[0, j] = pt[:, j * KEY_BLOCK:(j + 1) * KEY_BLOCK].astype(BF16)


def _in_proj_prompt(x, g1, w, wkvt, cw, cb, lg, lb):
    bsz, t, d = x.shape
    tm = ROW_TILE
    per_tile = tm // KEY_BLOCK
    row = lambda b, i: (b, i, 0)
    const2 = lambda b, i: (0, 0)
    const3 = lambda b, i: (0, 0, 0)
    act = lambda width: pl.BlockSpec((1, tm, width), row)
    vec = lambda a: pl.BlockSpec(a.shape, const2)
    kv_f = pl.BlockSpec((1, 1, N_HEADS, HEAD_DIM, tm), lambda b, i: (0, b, 0, 0, i))
    kv_b = pl.BlockSpec((1, per_tile, D_A, KEY_BLOCK), lambda b, i: (b, i, 0, 0))
    resident = lambda a: pl.BlockSpec(a.shape, const3, pipeline_mode=pl.Buffered(1))
    return pl.pallas_call(
        _in_proj_prompt_kernel,
        grid=(bsz, t // tm),
        in_specs=[act(d), vec(g1), pl.BlockSpec(w.shape, const2, pipeline_mode=pl.Buffered(1)),
                  resident(wkvt), vec(cw), vec(cb), vec(lg), vec(lb)],
        out_specs=[act(D_A), kv_b, kv_b, kv_f, kv_f, act(D_C),
                   pl.BlockSpec((1, HALO, D_C), lambda b, i: (b, 0, 0))],
        out_shape=[jax.ShapeDtypeStruct((bsz, t, D_A), BF16)]
        + [jax.ShapeDtypeStruct((bsz, t // KEY_BLOCK, D_A, KEY_BLOCK), BF16)] * 2
        + [jax.ShapeDtypeStruct((1, bsz, N_HEADS, HEAD_DIM, t), F32)] * 2
        + [jax.ShapeDtypeStruct((bsz, t, D_C), BF16), jax.ShapeDtypeStruct((bsz, HALO, D_C), F32)],
        scratch_shapes=[pltpu.VMEM((HALO, D_C), F32)],
        compiler_params=pltpu.CompilerParams(
            dimension_semantics=("arbitrary", "arbitrary"), vmem_limit_bytes=VMEM_LIMIT),
        name="in_proj_prompt",
    )(x, g1, w, wkvt, cw, cb, lg, lb)


def _in_proj_sample_kernel(x_ref, g1_ref, w_ref, hist_ref, cw_ref, cb_ref, lg_ref, lb_ref,
                           q_ref, kb_ref, vb_ref, kf_ref, vf_ref, c_ref, u_ref):
    nb, ts, d = x_ref.shape
    m = nb * ts
    x = x_ref[...].reshape(m, d)
    h = (x * _rms_scale(x) * g1_ref[...]).astype(BF16)
    q_ref[...] = (jnp.dot(h, w_ref[:, Q_COLS], preferred_element_type=F32) * Q_SCALE
                  ).astype(BF16).reshape(nb, ts, D_A)
    for c, b_ref, f_ref in ((0, kb_ref, kf_ref), (1, vb_ref, vf_ref)):
        p = jnp.dot(h, w_ref[:, KV_COLS[c]], preferred_element_type=F32)
        b_ref[...] = p.astype(BF16).reshape(nb, ts, D_A)
        for hd in range(N_HEADS):
            f_ref[0, :, hd, :, :] = p[:, hd * HEAD_DIM:(hd + 1) * HEAD_DIM].reshape(nb, ts, HEAD_DIM)
    a = jnp.dot(h, w_ref[:, A_COLS], preferred_element_type=F32)
    g = jnp.dot(h, w_ref[:, G_COLS], preferred_element_type=F32)
    u = (a * _sigmoid(g)).reshape(nb, ts, D_C)
    u_ref[...] = u
    ext = jnp.concatenate([hist_ref[...], u], axis=1)
    conv = _causal_dwconv(ext, cw_ref, cb_ref, ts).reshape(m, D_C)
    c_ref[...] = _conv_act(conv, lg_ref, lb_ref).reshape(nb, ts, D_C)


def _in_proj_sample(x, g1, w, hist, cw, cb, lg, lb):
    bsz, ts, d = x.shape
    args = (x, g1, w, hist, cw, cb, lg, lb)
    full = lambda shape: pl.BlockSpec(shape, lambda i: (0,) * len(shape))
    act = (bsz, ts, D_A)
    kvf = (1, bsz, N_HEADS, ts, HEAD_DIM)
    return pl.pallas_call(
        _in_proj_sample_kernel,
        grid=(1,),
        in_specs=[full(a.shape) for a in args],
        out_specs=[full(act), full(act), full(act), full(kvf), full(kvf), full(act), full(act)],
        out_shape=[jax.ShapeDtypeStruct(act, BF16)] * 3
        + [jax.ShapeDtypeStruct(kvf, F32)] * 2
        + [jax.ShapeDtypeStruct((bsz, ts, D_C), BF16), jax.ShapeDtypeStruct((bsz, ts, D_C), F32)],
        compiler_params=pltpu.CompilerParams(
            dimension_semantics=("arbitrary",), vmem_limit_bytes=VMEM_LIMIT),
        name="in_proj_sample",
    )(*args)


def _softplus2(z):
    return jnp.maximum(jnp.log(1.0 + jnp.exp2(jnp.minimum(z, SOFTPLUS_SATURATION))) * LOG2E, z)


def _split_heads(x2):
    lane = lax.broadcasted_iota(jnp.int32, x2.shape, 1)
    zero = jnp.zeros_like(x2)
    return jnp.where(lane < HEAD_DIM, x2, zero), jnp.where(lane < HEAD_DIM, zero, x2)


def _head_norm(o, g, ones):
    sq = o * o
    hi = sq.astype(BF16)
    lo = (sq - hi.astype(F32)).astype(BF16)
    ms = jnp.dot(hi, ones, preferred_element_type=F32) + jnp.dot(lo, ones, preferred_element_type=F32)
    return o * lax.rsqrt(ms + EPS) * g


def _sb_prompt_kernel(q_ref, kt_ref, vt_ref, tri_ref, ones_ref, g_ref, o_ref, acc_ref, carry_ref):
    i = pl.program_id(1)
    tq = q_ref.shape[1]
    tri = tri_ref[...]
    row = lax.broadcasted_iota(jnp.int32, (tq, KEY_BLOCK), 0)
    col = lax.broadcasted_iota(jnp.int32, (tq, KEY_BLOCK), 1)
    diag = col < row
    qs = [_split_heads(q_ref[0, :, p * LANES:(p + 1) * LANES]) for p in range(PAIRS)]

    def visit(blocks, fresh):
        heads = [(p, hh) for p in range(PAIRS) for hh in range(2)]
        zs = {}
        for n, (kb, mask) in enumerate(blocks):
            for hd, (p, hh) in enumerate(heads):
                kt2 = kt_ref[0, kb, p * LANES:(p + 1) * LANES, :]
                z = jnp.dot(qs[p][hh], kt2, preferred_element_type=F32)
                zs[n, hd] = z if mask is None else jnp.where(mask, z, MASKED)
        sps = {key: _softplus2(z).astype(BF16) for key, z in zs.items()}
        locs = {key: jnp.dot(sp, tri, preferred_element_type=F32) for key, sp in sps.items()}
        ws = {}
        lowest = None
        for hd in range(N_HEADS):
            carry = None if fresh else carry_ref[hd]
            for n in range(len(blocks)):
                total = locs[n, hd] if carry is None else locs[n, hd] + carry
                ws[n, hd] = jnp.exp2(zs[n, hd] - total).astype(BF16)
                carry = locs[n, hd][:, :1] if carry is None else carry + locs[n, hd][:, :1]
            carry_ref[hd] = carry
            lowest = carry if lowest is None else jnp.minimum(lowest, carry)
        first = lax.broadcasted_iota(jnp.int32, (tq, LANES), 1) < HEAD_DIM
        for p in range(PAIRS):
            pair = None
            for n, (kb, _) in enumerate(blocks):
                vt2 = vt_ref[0, kb, p * LANES:(p + 1) * LANES, :]
                pvs = [lax.dot_general(ws[n, 2 * p + hh], vt2, NT, preferred_element_type=F32)
                       for hh in range(2)]
                part = jnp.where(first, pvs[0], pvs[1])
                pair = part if pair is None else pair + part
            if fresh:
                acc_ref[p] = pair
            else:
                acc_ref[p] += pair
        return jnp.min(lowest)

    lowest = lax.cond(i == 0,
                      lambda: visit([(i, diag)], True),
                      lambda: visit([(i, diag), (i - 1, None)], True))

    def more(state):
        j, lowest = state
        return jnp.logical_and(j < i, lowest < UNDERFLOW_BITS)

    def body(state):
        j, _ = state
        return j + 1, visit([(i - 1 - j, None)], False)

    lax.while_loop(more, body, (jnp.int32(1), lowest))

    for p in range(PAIRS):
        sl = slice(p * LANES, (p + 1) * LANES)
        o_ref[0, :, sl] = _head_norm(acc_ref[p], g_ref[:, sl], ones_ref[...]).astype(BF16)


def _sb_prompt(q, kt, vt, tri, ones, g):
    bsz, t, _ = q.shape
    tq = KEY_BLOCK
    q_spec = pl.BlockSpec((1, tq, D_A), lambda b, i: (b, i, 0))
    kv_spec = pl.BlockSpec((1,) + kt.shape[1:], lambda b, i: (b, 0, 0, 0))
    return pl.pallas_call(
        _sb_prompt_kernel,
        grid=(bsz, t // tq),
        in_specs=[q_spec, kv_spec, kv_spec,
                  pl.BlockSpec(tri.shape, lambda b, i: (0, 0)),
                  pl.BlockSpec(ones.shape, lambda b, i: (0, 0)),
                  pl.BlockSpec((1, D_A), lambda b, i: (0, 0))],
        out_specs=q_spec,
        out_shape=jax.ShapeDtypeStruct((bsz, t, D_A), BF16),
        scratch_shapes=[pltpu.VMEM((PAIRS, tq, LANES), F32), pltpu.VMEM((N_HEADS, tq, 1), F32)],
        compiler_params=pltpu.CompilerParams(
            dimension_semantics=("arbitrary", "arbitrary"), vmem_limit_bytes=VMEM_LIMIT),
        name="sb_prompt",
    )(q, kt, vt, tri, ones, g)


def _sb_sample_kernel(q_ref, kn_ref, vn_ref, ckt_ref, cvt_ref, tri_ref, ones_ref, g_ref, o_ref):
    ts = q_ref.shape[1]
    past = ckt_ref.shape[4]
    nblk = past // KEY_BLOCK
    tri = tri_ref[...]
    row = lax.broadcasted_iota(jnp.int32, (ts, ts), 0)
    col = lax.broadcasted_iota(jnp.int32, (ts, ts), 1)
    causal = col < row
    heads = range(SAMPLE_HEADS)
    cols = [slice(hd * HEAD_DIM, (hd + 1) * HEAD_DIM) for hd in heads]
    qs = [q_ref[0][:, sl] for sl in cols]

    z_c = [jnp.dot(qs[hd], ckt_ref[0, 0, hd].astype(BF16), preferred_element_type=F32) for hd in heads]
    z_n = [jnp.where(causal, lax.dot_general(qs[hd], kn_ref[0][:, cols[hd]], NT,
                                             preferred_element_type=F32), MASKED) for hd in heads]
    c_n = [jnp.dot(_softplus2(z).astype(BF16), tri[:ts, :ts], preferred_element_type=F32) for z in z_n]
    sps = [_softplus2(z).astype(BF16) for z in z_c]
    locs = [jnp.dot(jnp.concatenate([sp[:, b * KEY_BLOCK:(b + 1) * KEY_BLOCK] for b in range(nblk)],
                                    axis=0), tri, preferred_element_type=F32) for sp in sps]
    outs = []
    for hd in heads:
        carry = c_n[hd][:, :1]
        cs = [None] * nblk
        for b in reversed(range(nblk)):
            loc = locs[hd][b * ts:(b + 1) * ts, :]
            cs[b] = loc + carry
            carry = carry + loc[:, :1]
        w_c = jnp.exp2(z_c[hd] - jnp.concatenate(cs, axis=1)).astype(BF16)
        w_n = jnp.exp2(z_n[hd] - c_n[hd]).astype(BF16)
        outs.append(jnp.dot(w_n, vn_ref[0][:, cols[hd]], preferred_element_type=F32)
                    + lax.dot_general(w_c, cvt_ref[0, 0, hd].astype(BF16), NT,
                                      preferred_element_type=F32))
    for p in range(SAMPLE_HEADS // 2):
        sl = slice(p * LANES, (p + 1) * LANES)
        o = jnp.concatenate(outs[2 * p:2 * p + 2], axis=-1)
        o_ref[0, :, sl] = _head_norm(o, g_ref[:, sl], ones_ref[...]).astype(BF16)


def _sb_sample(q, kn, vn, cache_kt, cache_vt, tri, ones, g):
    bsz, ts, _ = q.shape
    past = cache_kt.shape[4]
    width = SAMPLE_HEADS * HEAD_DIM
    new = pl.BlockSpec((1, ts, width), lambda b, p: (b, 0, p))
    cache = pl.BlockSpec((1, 1, SAMPLE_HEADS, HEAD_DIM, past), lambda b, p: (0, b, p, 0, 0))
    return pl.pallas_call(
        _sb_sample_kernel,
        grid=(bsz, N_HEADS // SAMPLE_HEADS),
        in_specs=[new, new, new, cache, cache,
                  pl.BlockSpec(tri.shape, lambda b, p: (0, 0)),
                  pl.BlockSpec(ones.shape, lambda b, p: (0, 0)),
                  pl.BlockSpec((1, width), lambda b, p: (0, p))],
        out_specs=new,
        out_shape=jax.ShapeDtypeStruct((bsz, ts, D_A), BF16),
        compiler_params=pltpu.CompilerParams(
            dimension_semantics=("arbitrary", "arbitrary"), vmem_limit_bytes=VMEM_LIMIT),
        name="sb_sample",
    )(q, kn, vn, cache_kt, cache_vt, tri, ones, g)


def _out_ffn_kernel(x_ref, o_ref, c_ref, wo_ref, g2_ref, wg_ref, wu_ref, wd_ref, gf_ref, y_ref,
                    acc_ref, hf_ref, ga_ref, gb_ref):
    nb, tm, d = x_ref.shape
    m = nb * tm
    n_chunks = wg_ref.shape[1] // FF_CHUNK
    assert n_chunks % 4 == 3 and n_chunks * FF_CHUNK == wg_ref.shape[1]
    mix = jnp.dot(o_ref[...].reshape(m, D_A), wo_ref[0], preferred_element_type=F32)
    mix = mix + jnp.dot(c_ref[...].reshape(m, D_C), wo_ref[1], preferred_element_type=F32)
    x1 = x_ref[...].reshape(m, d) + mix
    hf_ref[...] = (x1 * _rms_scale(x1) * g2_ref[...]).astype(BF16)
    acc_ref[...] = x1

    big = 2 * FF_CHUNK
    n_big = wg_ref.shape[1] // big
    rest = pl.ds(n_big * big, FF_CHUNK)

    def chunk(c):
        return pl.ds(pl.multiple_of(c * big, big), big)

    def gate_up(cols, gu_ref, width):
        hf = hf_ref[...]
        gu_ref[0, :, :width] = jnp.dot(hf, wg_ref[:, cols], preferred_element_type=F32)
        gu_ref[1, :, :width] = jnp.dot(hf, wu_ref[:, cols], preferred_element_type=F32)

    def down(rows, gu_ref, width):
        gate = gu_ref[0, :, :width]
        act = (gate * _sigmoid(gate) * gu_ref[1, :, :width]).astype(BF16)
        acc_ref[...] += jnp.dot(act, wd_ref[rows, :], preferred_element_type=F32)

    gate_up(chunk(0), ga_ref, big)

    def pair(j, _):
        c = 2 * j + 1
        down(chunk(c - 1), ga_ref, big)
        gate_up(chunk(c), gb_ref, big)
        down(chunk(c), gb_ref, big)
        gate_up(chunk(c + 1), ga_ref, big)
        return 0

    lax.fori_loop(0, (n_big - 1) // 2, pair, 0)
    gate_up(rest, gb_ref, FF_CHUNK)
    down(chunk(n_big - 1), ga_ref, big)
    down(rest, gb_ref, FF_CHUNK)
    x2 = acc_ref[...]
    y_ref[...] = (x2 * _rms_scale(x2) * gf_ref[...]).reshape(nb, tm, d)


def _out_ffn(x, o, c, wo2, g2, wg, wu, wd, gf, nb, tm):
    bsz, t, d = x.shape
    row = lambda b, i: (b, i, 0)
    act = lambda width: pl.BlockSpec((nb, tm, width), row)
    vec = lambda a: pl.BlockSpec(a.shape, lambda b, i: (0, 0))
    resident = lambda a: pl.BlockSpec(a.shape, lambda b, i: (0,) * a.ndim, pipeline_mode=pl.Buffered(1))
    return pl.pallas_call(
        _out_ffn_kernel,
        grid=(bsz // nb, t // tm),
        in_specs=[act(d), act(D_A), act(D_C),
                  resident(wo2), vec(g2), resident(wg), resident(wu), resident(wd), vec(gf)],
        out_specs=act(d),
        out_shape=jax.ShapeDtypeStruct((bsz, t, d), F32),
        scratch_shapes=[pltpu.VMEM((nb * tm, d), F32), pltpu.VMEM((nb * tm, d), BF16)]
        + [pltpu.VMEM((2, nb * tm, 2 * FF_CHUNK), F32)] * 2,
        compiler_params=pltpu.CompilerParams(
            dimension_semantics=("arbitrary", "arbitrary"), vmem_limit_bytes=VMEM_LIMIT),
        name="out_ffn",
    )(x, o, c, wo2, g2, wg, wu, wd, gf)


def _row(a):
    return a.reshape(1, -1)


def kernel(x_prompt, x_sample, cache_k, cache_v, state_conv, w_in, sb_norm_g, conv_w, conv_b,
           conv_ln_g, conv_ln_b, w_out, norm1_g, norm2_g, w_gate, w_up, w_down, final_g):
    assert w_in.shape[0] == 1, "single-layer step"
    d = x_prompt.shape[-1]

    w = w_in[0].astype(BF16)
    wkvt = w[:, D_A:3 * D_A].T.reshape(2, D_A, d)
    wo2 = w_out[0].astype(BF16).reshape(2, D_A, d)
    wg, wu, wd = w_gate[0].astype(BF16), w_up[0].astype(BF16), w_down[0].astype(BF16)
    g1, g2, gf = _row(norm1_g[0]), _row(norm2_g[0]), _row(final_g)
    sbg = _row(sb_norm_g[0])
    cw, cb = conv_w[0], _row(conv_b[0])
    lg, lb = _row(conv_ln_g[0]), _row(conv_ln_b[0])
    idx = jnp.arange(KEY_BLOCK)
    tri = (idx[:, None] >= idx[None, :]).astype(BF16)
    lane = jnp.arange(LANES) // HEAD_DIM
    ones = ((lane[:, None] == lane[None, :]) * (1.0 / HEAD_DIM)).astype(BF16)
    conv_params = (cw, cb, lg, lb)
    tail = (wo2, g2, wg, wu, wd, gf)

    q, ktb, vtb, ktf, vtf, c, u_tail = _in_proj_prompt(x_prompt, g1, w, wkvt, *conv_params)
    o = _sb_prompt(q, ktb, vtb, tri, ones, sbg)
    y_prompt = _out_ffn(x_prompt, o, c, *tail, 1, ROW_TILE)
    k_prompt = jnp.swapaxes(ktf, 3, 4)
    v_prompt = jnp.swapaxes(vtf, 3, 4)
    conv_prompt = u_tail[None, :, HALO - CONV_HIST:, :]

    bs, ts, _ = x_sample.shape
    hist = jnp.pad(state_conv[0], ((0, 0), (HALO - CONV_HIST, 0), (0, 0)))
    qs, kbs, vbs, k_sample, v_sample, cs, us = _in_proj_sample(x_sample, g1, w, hist, *conv_params)
    os_ = _sb_sample(qs, kbs, vbs, jnp.swapaxes(cache_k, 3, 4), jnp.swapaxes(cache_v, 3, 4), tri, ones, sbg)
    y_sample = _out_ffn(x_sample, os_, cs, *tail, bs, ts)
    conv_sample = jnp.concatenate([state_conv[0], us], axis=1)[None, :, -CONV_HIST:, :]

    return (y_prompt, y_sample, k_prompt, v_prompt, conv_prompt, k_sample, v_sample, conv_sample)
```

```python
import jax
import jax.numpy as jnp
from jax import lax
from jax.experimental import pallas as pl
from jax.experimental.pallas import tpu as pltpu

N_HEADS = 8
HEAD_DIM = 64
D_A = N_HEADS * HEAD_DIM
D_C = 512
CONV_W = 31
CONV_HIST = CONV_W - 1
EPS = 1e-6

SUBLANES = 8
LANES = 128
PAIRS = D_A // LANES
HALO = 32
KEY_BLOCK = 256
SAMPLE_HEADS = 4
FF_CHUNK = 256
ROW_TILE = 512
FFN_TILE = 1024
VMEM_LIMIT = 60000 * 1024

LOG2E = 1.4426950408889634
Q_SCALE = HEAD_DIM ** -0.5 * LOG2E
MASKED = -1e30
UNDERFLOW_BITS = 150.0
SOFTPLUS_SATURATION = 100.0

BF16 = jnp.bfloat16
F32 = jnp.float32

NT = (((1,), (1,)), ((), ()))

Q_COLS = slice(0, D_A)
KV_COLS = (slice(D_A, 2 * D_A), slice(2 * D_A, 3 * D_A))
A_COLS = slice(3 * D_A, 3 * D_A + D_C)
G_COLS = slice(3 * D_A + D_C, 3 * D_A + 2 * D_C)


def _rms_scale(x):
    return lax.rsqrt(jnp.mean(x * x, axis=-1, keepdims=True) + EPS)


def _sigmoid(x):
    return 0.5 + 0.5 * jnp.tanh(0.5 * x)


def _causal_dwconv(ext, cw_ref, cb_ref, tm):
    nb, rows, _ = ext.shape
    conv = jnp.zeros((nb, tm, D_C), F32) + cb_ref[...]
    first = HALO - CONV_HIST
    for r in range(SUBLANES):
        offsets = [m for m in range(first, first + CONV_W) if m % SUBLANES == r]
        shifted = ext if r == 0 else pltpu.roll(ext, rows - r, 1)
        for m in offsets:
            tap = m - first
            conv = conv + shifted[:, m - r:m - r + tm, :] * cw_ref[tap:tap + 1, :]
    return conv


def _conv_act(conv, lg_ref, lb_ref):
    mu = jnp.mean(conv, axis=-1, keepdims=True)
    cc = conv - mu
    ln = cc * lax.rsqrt(jnp.mean(cc * cc, axis=-1, keepdims=True) + EPS) * lg_ref[...] + lb_ref[...]
    return (ln * _sigmoid(ln)).astype(BF16)


def _in_proj_prompt_kernel(x_ref, g1_ref, w_ref, wkvt_ref, cw_ref, cb_ref, lg_ref, lb_ref,
                           q_ref, kb_ref, vb_ref, kf_ref, vf_ref, c_ref, tail_ref, hist_ref):
    tm = x_ref.shape[1]

    @pl.when(pl.program_id(1) == 0)
    def _():
        hist_ref[...] = jnp.zeros_like(hist_ref)

    x = x_ref[0]
    h = (x * _rms_scale(x) * g1_ref[...]).astype(BF16)
    a = jnp.dot(h, w_ref[:, A_COLS], preferred_element_type=F32)
    g = jnp.dot(h, w_ref[:, G_COLS], preferred_element_type=F32)
    u = a * _sigmoid(g)
    ext = jnp.concatenate([hist_ref[...], u], axis=0)
    c_ref[0] = _conv_act(_causal_dwconv(ext[None], cw_ref, cb_ref, tm)[0], lg_ref, lb_ref)
    hist_ref[...] = u[tm - HALO:, :]
    tail_ref[0] = u[tm - HALO:, :]

    q_ref[0] = (jnp.dot(h, w_ref[:, Q_COLS], preferred_element_type=F32) * Q_SCALE).astype(BF16)
    for c, b_ref, f_ref in ((0, kb_ref, kf_ref), (1, vb_ref, vf_ref)):
        pt = lax.dot_general(wkvt_ref[c], h, NT, preferred_element_type=F32)
        f_ref[0, 0] = pt.reshape(N_HEADS, HEAD_DIM, tm)
        for j in range(tm // KEY_BLOCK):
            b_ref[0, j] = pt[:, j * KEY_BLOCK:(j + 1) * KEY_BLOCK].astype(BF16)


def _in_proj_prompt(x, g1, w, wkvt, cw, cb, lg, lb):
    bsz, t, d = x.shape
    tm = ROW_TILE
    per_tile = tm // KEY_BLOCK
    row = lambda b, i: (b, i, 0)
    const2 = lambda b, i: (0, 0)
    const3 = lambda b, i: (0, 0, 0)
    act = lambda width: pl.BlockSpec((1, tm, width), row)
    vec = lambda a: pl.BlockSpec(a.shape, const2)
    kv_f = pl.BlockSpec((1, 1, N_HEADS, HEAD_DIM, tm), lambda b, i: (0, b, 0, 0, i))
    kv_b = pl.BlockSpec((1, per_tile, D_A, KEY_BLOCK), lambda b, i: (b, i, 0, 0))
    resident = lambda a: pl.BlockSpec(a.shape, const3, pipeline_mode=pl.Buffered(1))
    return pl.pallas_call(
        _in_proj_prompt_kernel,
        grid=(bsz, t // tm),
        in_specs=[act(d), vec(g1), pl.BlockSpec(w.shape, const2, pipeline_mode=pl.Buffered(1)),
                  resident(wkvt), vec(cw), vec(cb), vec(lg), vec(lb)],
        out_specs=[act(D_A), kv_b, kv_b, kv_f, kv_f, act(D_C),
                   pl.BlockSpec((1, HALO, D_C), lambda b, i: (b, 0, 0))],
        out_shape=[jax.ShapeDtypeStruct((bsz, t, D_A), BF16)]
        + [jax.ShapeDtypeStruct((bsz, t // KEY_BLOCK, D_A, KEY_BLOCK), BF16)] * 2
        + [jax.ShapeDtypeStruct((1, bsz, N_HEADS, HEAD_DIM, t), F32)] * 2
        + [jax.ShapeDtypeStruct((bsz, t, D_C), BF16), jax.ShapeDtypeStruct((bsz, HALO, D_C), F32)],
        scratch_shapes=[pltpu.VMEM((HALO, D_C), F32)],
        compiler_params=pltpu.CompilerParams(
            dimension_semantics=("arbitrary", "arbitrary"), vmem_limit_bytes=VMEM_LIMIT),
        name="in_proj_prompt",
    )(x, g1, w, wkvt, cw, cb, lg, lb)


def _in_proj_sample_kernel(x_ref, g1_ref, w_ref, hist_ref, cw_ref, cb_ref, lg_ref, lb_ref,
                           q_ref, kb_ref, vb_ref, kf_ref, vf_ref, c_ref, u_ref):
    nb, ts, d = x_ref.shape
    m = nb * ts
    x = x_ref[...].reshape(m, d)
    h = (x * _rms_scale(x) * g1_ref[...]).astype(BF16)
    q_ref[...] = (jnp.dot(h, w_ref[:, Q_COLS], preferred_element_type=F32) * Q_SCALE
                  ).astype(BF16).reshape(nb, ts, D_A)
    for c, b_ref, f_ref in ((0, kb_ref, kf_ref), (1, vb_ref, vf_ref)):
        p = jnp.dot(h, w_ref[:, KV_COLS[c]], preferred_element_type=F32)
        b_ref[...] = p.astype(BF16).reshape(nb, ts, D_A)
        for hd in range(N_HEADS):
            f_ref[0, :, hd, :, :] = p[:, hd * HEAD_DIM:(hd + 1) * HEAD_DIM].reshape(nb, ts, HEAD_DIM)
    a = jnp.dot(h, w_ref[:, A_COLS], preferred_element_type=F32)
    g = jnp.dot(h, w_ref[:, G_COLS], preferred_element_type=F32)
    u = (a * _sigmoid(g)).reshape(nb, ts, D_C)
    u_ref[...] = u
    ext = jnp.concatenate([hist_ref[...], u], axis=1)
    conv = _causal_dwconv(ext, cw_ref, cb_ref, ts).reshape(m, D_C)
    c_ref[...] = _conv_act(conv, lg_ref, lb_ref).reshape(nb, ts, D_C)


def _in_proj_sample(x, g1, w, hist, cw, cb, lg, lb):
    bsz, ts, d = x.shape
    args = (x, g1, w, hist, cw, cb, lg, lb)
    full = lambda shape: pl.BlockSpec(shape, lambda i: (0,) * len(shape))
    act = (bsz, ts, D_A)
    kvf = (1, bsz, N_HEADS, ts, HEAD_DIM)
    return pl.pallas_call(
        _in_proj_sample_kernel,
        grid=(1,),
        in_specs=[full(a.shape) for a in args],
        out_specs=[full(act), full(act), full(act), full(kvf), full(kvf), full(act), full(act)],
        out_shape=[jax.ShapeDtypeStruct(act, BF16)] * 3
        + [jax.ShapeDtypeStruct(kvf, F32)] * 2
        + [jax.ShapeDtypeStruct((bsz, ts, D_C), BF16), jax.ShapeDtypeStruct((bsz, ts, D_C), F32)],
        compiler_params=pltpu.CompilerParams(
            dimension_semantics=("arbitrary",), vmem_limit_bytes=VMEM_LIMIT),
        name="in_proj_sample",
    )(*args)


def _softplus2(z):
    return jnp.maximum(jnp.log(1.0 + jnp.exp2(jnp.minimum(z, SOFTPLUS_SATURATION))) * LOG2E, z)


def _split_heads(x2):
    lane = lax.broadcasted_iota(jnp.int32, x2.shape, 1)
    zero = jnp.zeros_like(x2)
    return jnp.where(lane < HEAD_DIM, x2, zero), jnp.where(lane < HEAD_DIM, zero, x2)


def _head_norm(o, g, ones):
    sq = o * o
    hi = sq.astype(BF16)
    lo = (sq - hi.astype(F32)).astype(BF16)
    ms = jnp.dot(hi, ones, preferred_element_type=F32) + jnp.dot(lo, ones, preferred_element_type=F32)
    return o * lax.rsqrt(ms + EPS) * g


def _sb_prompt_kernel(q_ref, kt_ref, vt_ref, tri_ref, ones_ref, g_ref, o_ref, acc_ref, carry_ref):
    i = pl.program_id(1)
    tq = q_ref.shape[1]
    tri = tri_ref[...]
    row = lax.broadcasted_iota(jnp.int32, (tq, KEY_BLOCK), 0)
    col = lax.broadcasted_iota(jnp.int32, (tq, KEY_BLOCK), 1)
    diag = col < row
    qs = [_split_heads(q_ref[0, :, p * LANES:(p + 1) * LANES]) for p in range(PAIRS)]

    def visit(blocks, fresh):
        heads = [(p, hh) for p in range(PAIRS) for hh in range(2)]
        zs = {}
        for n, (kb, mask) in enumerate(blocks):
            for hd, (p, hh) in enumerate(heads):
                kt2 = kt_ref[0, kb, p * LANES:(p + 1) * LANES, :]
                z = jnp.dot(qs[p][hh], kt2, preferred_element_type=F32)
                zs[n, hd] = z if mask is None else jnp.where(mask, z, MASKED)
        sps = {key: _softplus2(z).astype(BF16) for key, z in zs.items()}
        locs = {key: jnp.dot(sp, tri, preferred_element_type=F32) for key, sp in sps.items()}
        ws = {}
        lowest = None
        for hd in range(N_HEADS):
            carry = None if fresh else carry_ref[hd]
            for n in range(len(blocks)):
                total = locs[n, hd] if carry is None else locs[n, hd] + carry
                ws[n, hd] = jnp.exp2(zs[n, hd] - total).astype(BF16)
                carry = locs[n, hd][:, :1] if carry is None else carry + locs[n, hd][:, :1]
            carry_ref[hd] = carry
            lowest = carry if lowest is None else jnp.minimum(lowest, carry)
        first = lax.broadcasted_iota(jnp.int32, (tq, LANES), 1) < HEAD_DIM
        for p in range(PAIRS):
            pair = None
            for n, (kb, _) in enumerate(blocks):
                vt2 = vt_ref[0, kb, p * LANES:(p + 1) * LANES, :]
                pvs = [lax.dot_general(ws[n, 2 * p + hh], vt2, NT, preferred_element_type=F32)
                       for hh in range(2)]
                part = jnp.where(first, pvs[0], pvs[1])
                pair = part if pair is None else pair + part
            if fresh:
                acc_ref[p] = pair
            else:
                acc_ref[p] += pair
        return jnp.min(lowest)

    lowest = lax.cond(i == 0,
                      lambda: visit([(i, diag)], True),
                      lambda: visit([(i, diag), (i - 1, None)], True))

    def more(state):
        j, lowest = state
        return jnp.logical_and(j < i, lowest < UNDERFLOW_BITS)

    def body(state):
        j, _ = state
        return j + 1, visit([(i - 1 - j, None)], False)

    lax.while_loop(more, body, (jnp.int32(1), lowest))

    for p in range(PAIRS):
        sl = slice(p * LANES, (p + 1) * LANES)
        o_ref[0, :, sl] = _head_norm(acc_ref[p], g_ref[:, sl], ones_ref[...]).astype(BF16)


def _sb_prompt(q, kt, vt, tri, ones, g):
    bsz, t, _ = q.shape
    tq = KEY_BLOCK
    q_spec = pl.BlockSpec((1, tq, D_A), lambda b, i: (b, i, 0))
    kv_spec = pl.BlockSpec((1,) + kt.shape[1:], lambda b, i: (b, 0, 0, 0))
    return pl.pallas_call(
        _sb_prompt_kernel,
        grid=(bsz, t // tq),
        in_specs=[q_spec, kv_spec, kv_spec,
                  pl.BlockSpec(tri.shape, lambda b, i: (0, 0)),
                  pl.BlockSpec(ones.shape, lambda b, i: (0, 0)),
                  pl.BlockSpec((1, D_A), lambda b, i: (0, 0))],
        out_specs=q_spec,
        out_shape=jax.ShapeDtypeStruct((bsz, t, D_A), BF16),
        scratch_shapes=[pltpu.VMEM((PAIRS, tq, LANES), F32), pltpu.VMEM((N_HEADS, tq, 1), F32)],
        compiler_params=pltpu.CompilerParams(
            dimension_semantics=("arbitrary", "arbitrary"), vmem_limit_bytes=VMEM_LIMIT),
        name="sb_prompt",
    )(q, kt, vt, tri, ones, g)


def _sb_sample_kernel(q_ref, kn_ref, vn_ref, ckt_ref, cvt_ref, tri_ref, ones_ref, g_ref, o_ref):
    ts = q_ref.shape[1]
    past = ckt_ref.shape[4]
    nblk = past // KEY_BLOCK
    tri = tri_ref[...]
    row = lax.broadcasted_iota(jnp.int32, (ts, ts), 0)
    col = lax.broadcasted_iota(jnp.int32, (ts, ts), 1)
    causal = col < row
    heads = range(SAMPLE_HEADS)
    cols = [slice(hd * HEAD_DIM, (hd + 1) * HEAD_DIM) for hd in heads]
    qs = [q_ref[0][:, sl] for sl in cols]

    z_c = [jnp.dot(qs[hd], ckt_ref[0, 0, hd].astype(BF16), preferred_element_type=F32) for hd in heads]
    z_n = [jnp.where(causal, lax.dot_general(qs[hd], kn_ref[0][:, cols[hd]], NT,
                                             preferred_element_type=F32), MASKED) for hd in heads]
    c_n = [jnp.dot(_softplus2(z).astype(BF16), tri[:ts, :ts], preferred_element_type=F32) for z in z_n]
    sps = [_softplus2(z).astype(BF16) for z in z_c]
    locs = [jnp.dot(jnp.concatenate([sp[:, b * KEY_BLOCK:(b + 1) * KEY_BLOCK] for b in range(nblk)],
                                    axis=0), tri, preferred_element_type=F32) for sp in sps]
    outs = []
    for hd in heads:
        carry = c_n[hd][:, :1]
        cs = [None] * nblk
        for b in reversed(range(nblk)):
            loc = locs[hd][b * ts:(b + 1) * ts, :]
            cs[b] = loc + carry
            carry = carry + loc[:, :1]
        w_c = jnp.exp2(z_c[hd] - jnp.concatenate(cs, axis=1)).astype(BF16)
        w_n = jnp.exp2(z_n[hd] - c_n[hd]).astype(BF16)
        outs.append(jnp.dot(w_n, vn_ref[0][:, cols[hd]], preferred_element_type=F32)
                    + lax.dot_general(w_c, cvt_ref[0, 0, hd].astype(BF16), NT,
                                      preferred_element_type=F32))
    for p in range(SAMPLE_HEADS // 2):
        sl = slice(p * LANES, (p + 1) * LANES)
        o = jnp.concatenate(outs[2 * p:2 * p + 2], axis=-1)
        o_ref[0, :, sl] = _head_norm(o, g_ref[:, sl], ones_ref[...]).astype(BF16)


def _sb_sample(q, kn, vn, cache_kt, cache_vt, tri, ones, g):
    bsz, ts, _ = q.shape
    past = cache_kt.shape[4]
    width = SAMPLE_HEADS * HEAD_DIM
    new = pl.BlockSpec((1, ts, width), lambda b, p: (b, 0, p))
    cache = pl.BlockSpec((1, 1, SAMPLE_HEADS, HEAD_DIM, past), lambda b, p: (0, b, p, 0, 0))
    return pl.pallas_call(
        _sb_sample_kernel,
        grid=(bsz, N_HEADS // SAMPLE_HEADS),
        in_specs=[new, new, new, cache, cache,
                  pl.BlockSpec(tri.shape, lambda b, p: (0, 0)),
                  pl.BlockSpec(ones.shape, lambda b, p: (0, 0)),
                  pl.BlockSpec((1, width), lambda b, p: (0, p))],
        out_specs=new,
        out_shape=jax.ShapeDtypeStruct((bsz, ts, D_A), BF16),
        compiler_params=pltpu.CompilerParams(
            dimension_semantics=("arbitrary", "arbitrary"), vmem_limit_bytes=VMEM_LIMIT),
        name="sb_sample",
    )(q, kn, vn, cache_kt, cache_vt, tri, ones, g)


def _out_ffn_kernel(x_ref, o_ref, c_ref, wo_ref, g2_ref, wg_ref, wu_ref, wd_ref, gf_ref, y_ref,
                    acc_ref, hf_ref, ga_ref, gb_ref):
    nb, tm, d = x_ref.shape
    m = nb * tm
    n_chunks = wg_ref.shape[1] // FF_CHUNK
    assert n_chunks % 4 == 3 and n_chunks * FF_CHUNK == wg_ref.shape[1]
    mix = jnp.dot(o_ref[...].reshape(m, D_A), wo_ref[0], preferred_element_type=F32)
    mix = mix + jnp.dot(c_ref[...].reshape(m, D_C), wo_ref[1], preferred_element_type=F32)
    x1 = x_ref[...].reshape(m, d) + mix
    hf_ref[...] = (x1 * _rms_scale(x1) * g2_ref[...]).astype(BF16)
    acc_ref[...] = x1

    big = 2 * FF_CHUNK
    n_big = wg_ref.shape[1] // big
    rest = pl.ds(n_big * big, FF_CHUNK)

    def chunk(c):
        return pl.ds(pl.multiple_of(c * big, big), big)

    def gate_up(cols, gu_ref, width):
        hf = hf_ref[...]
        gu_ref[0, :, :width] = jnp.dot(hf, wg_ref[:, cols], preferred_element_type=F32)
        gu_ref[1, :, :width] = jnp.dot(hf, wu_ref[:, cols], preferred_element_type=F32)

    def down(rows, gu_ref, width):
        gate = gu_ref[0, :, :width]
        act = (gate * _sigmoid(gate) * gu_ref[1, :, :width]).astype(BF16)
        acc_ref[...] += jnp.dot(act, wd_ref[rows, :], preferred_element_type=F32)

    gate_up(chunk(0), ga_ref, big)

    def pair(j, _):
        c = 2 * j + 1
        down(chunk(c - 1), ga_ref, big)
        gate_up(chunk(c), gb_ref, big)
        down(chunk(c), gb_ref, big)
        gate_up(chunk(c + 1), ga_ref, big)
        return 0

    lax.fori_loop(0, (n_big - 1) // 2, pair, 0)
    gate_up(rest, gb_ref, FF_CHUNK)
    down(chunk(n_big - 1), ga_ref, big)
    down(rest, gb_ref, FF_CHUNK)
    x2 = acc_ref[...]
    y_ref[...] = (x2 * _rms_scale(x2) * gf_ref[...]).reshape(nb, tm, d)


def _out_ffn(x, o, c, wo2, g2, wg, wu, wd, gf, nb, tm):
    bsz, t, d = x.shape
    row = lambda b, i: (b, i, 0)
    act = lambda width: pl.BlockSpec((nb, tm, width), row)
    vec = lambda a: pl.BlockSpec(a.shape, lambda b, i: (0, 0))
    resident = lambda a: pl.BlockSpec(a.shape, lambda b, i: (0,) * a.ndim, pipeline_mode=pl.Buffered(1))
    return pl.pallas_call(
        _out_ffn_kernel,
        grid=(bsz // nb, t // tm),
        in_specs=[act(d), act(D_A), act(D_C),
                  resident(wo2), vec(g2), resident(wg), resident(wu), resident(wd), vec(gf)],
        out_specs=act(d),
        out_shape=jax.ShapeDtypeStruct((bsz, t, d), F32),
        scratch_shapes=[pltpu.VMEM((nb * tm, d), F32), pltpu.VMEM((nb * tm, d), BF16)]
        + [pltpu.VMEM((2, nb * tm, 2 * FF_CHUNK), F32)] * 2,
        compiler_params=pltpu.CompilerParams(
            dimension_semantics=("arbitrary", "arbitrary"), vmem_limit_bytes=VMEM_LIMIT),
        name="out_ffn",
    )(x, o, c, wo2, g2, wg, wu, wd, gf)


def _row(a):
    return a.reshape(1, -1)


def kernel(x_prompt, x_sample, cache_k, cache_v, state_conv, w_in, sb_norm_g, conv_w, conv_b,
           conv_ln_g, conv_ln_b, w_out, norm1_g, norm2_g, w_gate, w_up, w_down, final_g):
    assert w_in.shape[0] == 1, "single-layer step"
    d = x_prompt.shape[-1]

    w = w_in[0].astype(BF16)
    wkvt = w[:, D_A:3 * D_A].T.reshape(2, D_A, d)
    wo2 = w_out[0].astype(BF16).reshape(2, D_A, d)
    wg, wu, wd = w_gate[0].astype(BF16), w_up[0].astype(BF16), w_down[0].astype(BF16)
    g1, g2, gf = _row(norm1_g[0]), _row(norm2_g[0]), _row(final_g)
    sbg = _row(sb_norm_g[0])
    cw, cb = conv_w[0], _row(conv_b[0])
    lg, lb = _row(conv_ln_g[0]), _row(conv_ln_b[0])
    idx = jnp.arange(KEY_BLOCK)
    tri = (idx[:, None] >= idx[None, :]).astype(BF16)
    lane = jnp.arange(LANES) // HEAD_DIM
    ones = ((lane[:, None] == lane[None, :]) * (1.0 / HEAD_DIM)).astype(BF16)
    conv_params = (cw, cb, lg, lb)
    tail = (wo2, g2, wg, wu, wd, gf)

    q, ktb, vtb, ktf, vtf, c, u_tail = _in_proj_prompt(x_prompt, g1, w, wkvt, *conv_params)
    o = _sb_prompt(q, ktb, vtb, tri, ones, sbg)
    y_prompt = _out_ffn(x_prompt, o, c, *tail, 1, FFN_TILE)
    k_prompt = jnp.swapaxes(ktf, 3, 4)
    v_prompt = jnp.swapaxes(vtf, 3, 4)
    conv_prompt = u_tail[None, :, HALO - CONV_HIST:, :]

    bs, ts, _ = x_sample.shape
    hist = jnp.pad(state_conv[0], ((0, 0), (HALO - CONV_HIST, 0), (0, 0)))
    qs, kbs, vbs, k_sample, v_sample, cs, us = _in_proj_sample(x_sample, g1, w, hist, *conv_params)
    os_ = _sb_sample(qs, kbs, vbs, jnp.swapaxes(cache_k, 3, 4), jnp.swapaxes(cache_v, 3, 4), tri, ones, sbg)
    y_sample = _out_ffn(x_sample, os_, cs, *tail, bs, ts)
    conv_sample = jnp.concatenate([state_conv[0], us], axis=1)[None, :, -CONV_HIST:, :]

    return (y_prompt, y_sample, k_prompt, v_prompt, conv_prompt, k_sample, v_sample, conv_sample)
```

```python
import jax
import jax.numpy as jnp
from jax import lax
from jax.experimental import pallas as pl
from jax.experimental.pallas import tpu as pltpu

N_HEADS = 8
HEAD_DIM = 64
D_A = N_HEADS * HEAD_DIM
D_C = 512
CONV_W = 31
CONV_HIST = CONV_W - 1
EPS = 1e-6

SUBLANES = 8
LANES = 128
PAIRS = D_A // LANES
HALO = 32
KEY_BLOCK = 256
SAMPLE_HEADS = 8
FF_CHUNK = 256
ROW_TILE = 512
FFN_TILE = 1024
VMEM_LIMIT = 60000 * 1024

LOG2E = 1.4426950408889634
Q_SCALE = HEAD_DIM ** -0.5 * LOG2E
MASKED = -1e30
UNDERFLOW_BITS = 150.0
SOFTPLUS_SATURATION = 100.0

BF16 = jnp.bfloat16
F32 = jnp.float32

NT = (((1,), (1,)), ((), ()))

Q_COLS = slice(0, D_A)
KV_COLS = (slice(D_A, 2 * D_A), slice(2 * D_A, 3 * D_A))
A_COLS = slice(3 * D_A, 3 * D_A + D_C)
G_COLS = slice(3 * D_A + D_C, 3 * D_A + 2 * D_C)


def _rms_scale(x):
    return lax.rsqrt(jnp.mean(x * x, axis=-1, keepdims=True) + EPS)


def _sigmoid(x):
    return 0.5 + 0.5 * jnp.tanh(0.5 * x)


def _causal_dwconv(ext, cw_ref, cb_ref, tm):
    nb, rows, _ = ext.shape
    conv = jnp.zeros((nb, tm, D_C), F32) + cb_ref[...]
    first = HALO - CONV_HIST
    for r in range(SUBLANES):
        offsets = [m for m in range(first, first + CONV_W) if m % SUBLANES == r]
        shifted = ext if r == 0 else pltpu.roll(ext, rows - r, 1)
        for m in offsets:
            tap = m - first
            conv = conv + shifted[:, m - r:m - r + tm, :] * cw_ref[tap:tap + 1, :]
    return conv


def _conv_act(conv, lg_ref, lb_ref):
    mu = jnp.mean(conv, axis=-1, keepdims=True)
    cc = conv - mu
    ln = cc * lax.rsqrt(jnp.mean(cc * cc, axis=-1, keepdims=True) + EPS) * lg_ref[...] + lb_ref[...]
    return (ln * _sigmoid(ln)).astype(BF16)


def _in_proj_prompt_kernel(x_ref, g1_ref, w_ref, wkvt_ref, cw_ref, cb_ref, lg_ref, lb_ref,
                           q_ref, kb_ref, vb_ref, kf_ref, vf_ref, c_ref, tail_ref, hist_ref):
    tm = x_ref.shape[1]

    @pl.when(pl.program_id(1) == 0)
    def _():
        hist_ref[...] = jnp.zeros_like(hist_ref)

    x = x_ref[0]
    h = (x * _rms_scale(x) * g1_ref[...]).astype(BF16)
    a = jnp.dot(h, w_ref[:, A_COLS], preferred_element_type=F32)
    g = jnp.dot(h, w_ref[:, G_COLS], preferred_element_type=F32)
    u = a * _sigmoid(g)
    ext = jnp.concatenate([hist_ref[...], u], axis=0)
    c_ref[0] = _conv_act(_causal_dwconv(ext[None], cw_ref, cb_ref, tm)[0], lg_ref, lb_ref)
    hist_ref[...] = u[tm - HALO:, :]
    tail_ref[0] = u[tm - HALO:, :]

    q_ref[0] = (jnp.dot(h, w_ref[:, Q_COLS], preferred_element_type=F32) * Q_SCALE).astype(BF16)
    for c, b_ref, f_ref in ((0, kb_ref, kf_ref), (1, vb_ref, vf_ref)):
        pt = lax.dot_general(wkvt_ref[c], h, NT, preferred_element_type=F32)
        f_ref[0, 0] = pt.reshape(N_HEADS, HEAD_DIM, tm)
        for j in range(tm // KEY_BLOCK):
            b_ref[0, j] = pt[:, j * KEY_BLOCK:(j + 1) * KEY_BLOCK].astype(BF16)


def _in_proj_prompt(x, g1, w, wkvt, cw, cb, lg, lb):
    bsz, t, d = x.shape
    tm = ROW_TILE
    per_tile = tm // KEY_BLOCK
    row = lambda b, i: (b, i, 0)
    const2 = lambda b, i: (0, 0)
    const3 = lambda b, i: (0, 0, 0)
    act = lambda width: pl.BlockSpec((1, tm, width), row)
    vec = lambda a: pl.BlockSpec(a.shape, const2)
    kv_f = pl.BlockSpec((1, 1, N_HEADS, HEAD_DIM, tm), lambda b, i: (0, b, 0, 0, i))
    kv_b = pl.BlockSpec((1, per_tile, D_A, KEY_BLOCK), lambda b, i: (b, i, 0, 0))
    resident = lambda a: pl.BlockSpec(a.shape, const3, pipeline_mode=pl.Buffered(1))
    return pl.pallas_call(
        _in_proj_prompt_kernel,
        grid=(bsz, t // tm),
        in_specs=[act(d), vec(g1), pl.BlockSpec(w.shape, const2, pipeline_mode=pl.Buffered(1)),
                  resident(wkvt), vec(cw), vec(cb), vec(lg), vec(lb)],
        out_specs=[act(D_A), kv_b, kv_b, kv_f, kv_f, act(D_C),
                   pl.BlockSpec((1, HALO, D_C), lambda b, i: (b, 0, 0))],
        out_shape=[jax.ShapeDtypeStruct((bsz, t, D_A), BF16)]
        + [jax.ShapeDtypeStruct((bsz, t // KEY_BLOCK, D_A, KEY_BLOCK), BF16)] * 2
        + [jax.ShapeDtypeStruct((1, bsz, N_HEADS, HEAD_DIM, t), F32)] * 2
        + [jax.ShapeDtypeStruct((bsz, t, D_C), BF16), jax.ShapeDtypeStruct((bsz, HALO, D_C), F32)],
        scratch_shapes=[pltpu.VMEM((HALO, D_C), F32)],
        compiler_params=pltpu.CompilerParams(
            dimension_semantics=("arbitrary", "arbitrary"), vmem_limit_bytes=VMEM_LIMIT),
        name="in_proj_prompt",
    )(x, g1, w, wkvt, cw, cb, lg, lb)


def _in_proj_sample_kernel(x_ref, g1_ref, w_ref, hist_ref, cw_ref, cb_ref, lg_ref, lb_ref,
                           q_ref, kb_ref, vb_ref, kf_ref, vf_ref, c_ref, u_ref):
    nb, ts, d = x_ref.shape
    m = nb * ts
    x = x_ref[...].reshape(m, d)
    h = (x * _rms_scale(x) * g1_ref[...]).astype(BF16)
    q_ref[...] = (jnp.dot(h, w_ref[:, Q_COLS], preferred_element_type=F32) * Q_SCALE
                  ).astype(BF16).reshape(nb, ts, D_A)
    for c, b_ref, f_ref in ((0, kb_ref, kf_ref), (1, vb_ref, vf_ref)):
        p = jnp.dot(h, w_ref[:, KV_COLS[c]], preferred_element_type=F32)
        b_ref[...] = p.astype(BF16).reshape(nb, ts, D_A)
        for hd in range(N_HEADS):
            f_ref[0, :, hd, :, :] = p[:, hd * HEAD_DIM:(hd + 1) * HEAD_DIM].reshape(nb, ts, HEAD_DIM)
    a = jnp.dot(h, w_ref[:, A_COLS], preferred_element_type=F32)
    g = jnp.dot(h, w_ref[:, G_COLS], preferred_element_type=F32)
    u = (a * _sigmoid(g)).reshape(nb, ts, D_C)
    u_ref[...] = u
    ext = jnp.concatenate([hist_ref[...], u], axis=1)
    conv = _causal_dwconv(ext, cw_ref, cb_ref, ts).reshape(m, D_C)
    c_ref[...] = _conv_act(conv, lg_ref, lb_ref).reshape(nb, ts, D_C)


def _in_proj_sample(x, g1, w, hist, cw, cb, lg, lb):
    bsz, ts, d = x.shape
    args = (x, g1, w, hist, cw, cb, lg, lb)
    full = lambda shape: pl.BlockSpec(shape, lambda i: (0,) * len(shape))
    act = (bsz, ts, D_A)
    kvf = (1, bsz, N_HEADS, ts, HEAD_DIM)
    return pl.pallas_call(
        _in_proj_sample_kernel,
        grid=(1,),
        in_specs=[full(a.shape) for a in args],
        out_specs=[full(act), full(act), full(act), full(kvf), full(kvf), full(act), full(act)],
        out_shape=[jax.ShapeDtypeStruct(act, BF16)] * 3
        + [jax.ShapeDtypeStruct(kvf, F32)] * 2
        + [jax.ShapeDtypeStruct((bsz, ts, D_C), BF16), jax.ShapeDtypeStruct((bsz, ts, D_C), F32)],
        compiler_params=pltpu.CompilerParams(
            dimension_semantics=("arbitrary",), vmem_limit_bytes=VMEM_LIMIT),
        name="in_proj_sample",
    )(*args)


def _softplus2(z):
    return jnp.maximum(jnp.log(1.0 + jnp.exp2(jnp.minimum(z, SOFTPLUS_SATURATION))) * LOG2E, z)


def _split_heads(x2):
    lane = lax.broadcasted_iota(jnp.int32, x2.shape, 1)
    zero = jnp.zeros_like(x2)
    return jnp.where(lane < HEAD_DIM, x2, zero), jnp.where(lane < HEAD_DIM, zero, x2)


def _head_norm(o, g, ones):
    sq = o * o
    hi = sq.astype(BF16)
    lo = (sq - hi.astype(F32)).astype(BF16)
    ms = jnp.dot(hi, ones, preferred_element_type=F32) + jnp.dot(lo, ones, preferred_element_type=F32)
    return o * lax.rsqrt(ms + EPS) * g


def _sb_prompt_kernel(q_ref, kt_ref, vt_ref, tri_ref, ones_ref, g_ref, o_ref, acc_ref, carry_ref):
    i = pl.program_id(1)
    tq = q_ref.shape[1]
    tri = tri_ref[...]
    row = lax.broadcasted_iota(jnp.int32, (tq, KEY_BLOCK), 0)
    col = lax.broadcasted_iota(jnp.int32, (tq, KEY_BLOCK), 1)
    diag = col < row
    qs = [_split_heads(q_ref[0, :, p * LANES:(p + 1) * LANES]) for p in range(PAIRS)]

    def visit(blocks, fresh):
        heads = [(p, hh) for p in range(PAIRS) for hh in range(2)]
        zs = {}
        for n, (kb, mask) in enumerate(blocks):
            for hd, (p, hh) in enumerate(heads):
                kt2 = kt_ref[0, kb, p * LANES:(p + 1) * LANES, :]
                z = jnp.dot(qs[p][hh], kt2, preferred_element_type=F32)
                zs[n, hd] = z if mask is None else jnp.where(mask, z, MASKED)
        sps = {key: _softplus2(z).astype(BF16) for key, z in zs.items()}
        locs = {key: jnp.dot(sp, tri, preferred_element_type=F32) for key, sp in sps.items()}
        ws = {}
        lowest = None
        for hd in range(N_HEADS):
            carry = None if fresh else carry_ref[hd]
            for n in range(len(blocks)):
                total = locs[n, hd] if carry is None else locs[n, hd] + carry
                ws[n, hd] = jnp.exp2(zs[n, hd] - total).astype(BF16)
                carry = locs[n, hd][:, :1] if carry is None else carry + locs[n, hd][:, :1]
            carry_ref[hd] = carry
            lowest = carry if lowest is None else jnp.minimum(lowest, carry)
        first = lax.broadcasted_iota(jnp.int32, (tq, LANES), 1) < HEAD_DIM
        for p in range(PAIRS):
            pair = None
            for n, (kb, _) in enumerate(blocks):
                vt2 = vt_ref[0, kb, p * LANES:(p + 1) * LANES, :]
                pvs = [lax.dot_general(ws[n, 2 * p + hh], vt2, NT, preferred_element_type=F32)
                       for hh in range(2)]
                part = jnp.where(first, pvs[0], pvs[1])
                pair = part if pair is None else pair + part
            if fresh:
                acc_ref[p] = pair
            else:
                acc_ref[p] += pair
        return jnp.min(lowest)

    lowest = lax.cond(i == 0,
                      lambda: visit([(i, diag)], True),
                      lambda: visit([(i, diag), (i - 1, None)], True))

    def more(state):
        j, lowest = state
        return jnp.logical_and(j < i, lowest < UNDERFLOW_BITS)

    def body(state):
        j, _ = state
        return j + 1, visit([(i - 1 - j, None)], False)

    lax.while_loop(more, body, (jnp.int32(1), lowest))

    for p in range(PAIRS):
        sl = slice(p * LANES, (p + 1) * LANES)
        o_ref[0, :, sl] = _head_norm(acc_ref[p], g_ref[:, sl], ones_ref[...]).astype(BF16)


def _sb_prompt(q, kt, vt, tri, ones, g):
    bsz, t, _ = q.shape
    tq = KEY_BLOCK
    q_spec = pl.BlockSpec((1, tq, D_A), lambda b, i: (b, i, 0))
    kv_spec = pl.BlockSpec((1,) + kt.shape[1:], lambda b, i: (b, 0, 0, 0))
    return pl.pallas_call(
        _sb_prompt_kernel,
        grid=(bsz, t // tq),
        in_specs=[q_spec, kv_spec, kv_spec,
                  pl.BlockSpec(tri.shape, lambda b, i: (0, 0)),
                  pl.BlockSpec(ones.shape, lambda b, i: (0, 0)),
                  pl.BlockSpec((1, D_A), lambda b, i: (0, 0))],
        out_specs=q_spec,
        out_shape=jax.ShapeDtypeStruct((bsz, t, D_A), BF16),
        scratch_shapes=[pltpu.VMEM((PAIRS, tq, LANES), F32), pltpu.VMEM((N_HEADS, tq, 1), F32)],
        compiler_params=pltpu.CompilerParams(
            dimension_semantics=("arbitrary", "arbitrary"), vmem_limit_bytes=VMEM_LIMIT),
        name="sb_prompt",
    )(q, kt, vt, tri, ones, g)


def _sb_sample_kernel(q_ref, kn_ref, vn_ref, ckt_ref, cvt_ref, tri_ref, ones_ref, g_ref, o_ref):
    ts = q_ref.shape[1]
    past = ckt_ref.shape[4]
    nblk = past // KEY_BLOCK
    tri = tri_ref[...]
    row = lax.broadcasted_iota(jnp.int32, (ts, ts), 0)
    col = lax.broadcasted_iota(jnp.int32, (ts, ts), 1)
    causal = col < row
    heads = range(SAMPLE_HEADS)
    cols = [slice(hd * HEAD_DIM, (hd + 1) * HEAD_DIM) for hd in heads]
    qs = [q_ref[0][:, sl] for sl in cols]

    z_c = [jnp.dot(qs[hd], ckt_ref[0, 0, hd].astype(BF16), preferred_element_type=F32) for hd in heads]
    z_n = [jnp.where(causal, lax.dot_general(qs[hd], kn_ref[0][:, cols[hd]], NT,
                                             preferred_element_type=F32), MASKED) for hd in heads]
    c_n = [jnp.dot(_softplus2(z).astype(BF16), tri[:ts, :ts], preferred_element_type=F32) for z in z_n]
    sps = [_softplus2(z).astype(BF16) for z in z_c]
    locs = [jnp.dot(jnp.concatenate([sp[:, b * KEY_BLOCK:(b + 1) * KEY_BLOCK] for b in range(nblk)],
                                    axis=0), tri, preferred_element_type=F32) for sp in sps]
    outs = []
    for hd in heads:
        carry = c_n[hd][:, :1]
        cs = [None] * nblk
        for b in reversed(range(nblk)):
            loc = locs[hd][b * ts:(b + 1) * ts, :]
            cs[b] = loc + carry
            carry = carry + loc[:, :1]
        w_c = jnp.exp2(z_c[hd] - jnp.concatenate(cs, axis=1)).astype(BF16)
        w_n = jnp.exp2(z_n[hd] - c_n[hd]).astype(BF16)
        outs.append(jnp.dot(w_n, vn_ref[0][:, cols[hd]], preferred_element_type=F32)
                    + lax.dot_general(w_c, cvt_ref[0, 0, hd].astype(BF16), NT,
                                      preferred_element_type=F32))
    for p in range(SAMPLE_HEADS // 2):
        sl = slice(p * LANES, (p + 1) * LANES)
        o = jnp.concatenate(outs[2 * p:2 * p + 2], axis=-1)
        o_ref[0, :, sl] = _head_norm(o, g_ref[:, sl], ones_ref[...]).astype(BF16)


def _sb_sample(q, kn, vn, cache_kt, cache_vt, tri, ones, g):
    bsz, ts, _ = q.shape
    past = cache_kt.shape[4]
    width = SAMPLE_HEADS * HEAD_DIM
    new = pl.BlockSpec((1, ts, width), lambda b, p: (b, 0, p))
    cache = pl.BlockSpec((1, 1, SAMPLE_HEADS, HEAD_DIM, past), lambda b, p: (0, b, p, 0, 0))
    return pl.pallas_call(
        _sb_sample_kernel,
        grid=(bsz, N_HEADS // SAMPLE_HEADS),
        in_specs=[new, new, new, cache, cache,
                  pl.BlockSpec(tri.shape, lambda b, p: (0, 0)),
                  pl.BlockSpec(ones.shape, lambda b, p: (0, 0)),
                  pl.BlockSpec((1, width), lambda b, p: (0, p))],
        out_specs=new,
        out_shape=jax.ShapeDtypeStruct((bsz, ts, D_A), BF16),
        compiler_params=pltpu.CompilerParams(
            dimension_semantics=("arbitrary", "arbitrary"), vmem_limit_bytes=VMEM_LIMIT),
        name="sb_sample",
    )(q, kn, vn, cache_kt, cache_vt, tri, ones, g)


def _out_ffn_kernel(x_ref, o_ref, c_ref, wo_ref, g2_ref, wg_ref, wu_ref, wd_ref, gf_ref, y_ref,
                    acc_ref, hf_ref, ga_ref, gb_ref):
    nb, tm, d = x_ref.shape
    m = nb * tm
    n_chunks = wg_ref.shape[1] // FF_CHUNK
    assert n_chunks % 4 == 3 and n_chunks * FF_CHUNK == wg_ref.shape[1]
    mix = jnp.dot(o_ref[...].reshape(m, D_A), wo_ref[0], preferred_element_type=F32)
    mix = mix + jnp.dot(c_ref[...].reshape(m, D_C), wo_ref[1], preferred_element_type=F32)
    x1 = x_ref[...].reshape(m, d) + mix
    hf_ref[...] = (x1 * _rms_scale(x1) * g2_ref[...]).astype(BF16)
    acc_ref[...] = x1

    big = 2 * FF_CHUNK
    n_big = wg_ref.shape[1] // big
    rest = pl.ds(n_big * big, FF_CHUNK)

    def chunk(c):
        return pl.ds(pl.multiple_of(c * big, big), big)

    def gate_up(cols, gu_ref, width):
        hf = hf_ref[...]
        gu_ref[0, :, :width] = jnp.dot(hf, wg_ref[:, cols], preferred_element_type=F32)
        gu_ref[1, :, :width] = jnp.dot(hf, wu_ref[:, cols], preferred_element_type=F32)

    def down(rows, gu_ref, width):
        gate = gu_ref[0, :, :width]
        act = (gate * _sigmoid(gate) * gu_ref[1, :, :width]).astype(BF16)
        acc_ref[...] += jnp.dot(act, wd_ref[rows, :], preferred_element_type=F32)

    gate_up(chunk(0), ga_ref, big)

    def pair(j, _):
        c = 2 * j + 1
        down(chunk(c - 1), ga_ref, big)
        gate_up(chunk(c), gb_ref, big)
        down(chunk(c), gb_ref, big)
        gate_up(chunk(c + 1), ga_ref, big)
        return 0

    lax.fori_loop(0, (n_big - 1) // 2, pair, 0)
    gate_up(rest, gb_ref, FF_CHUNK)
    down(chunk(n_big - 1), ga_ref, big)
    down(rest, gb_ref, FF_CHUNK)
    x2 = acc_ref[...]
    y_ref[...] = (x2 * _rms_scale(x2) * gf_ref[...]).reshape(nb, tm, d)


def _out_ffn(x, o, c, wo2, g2, wg, wu, wd, gf, nb, tm):
    bsz, t, d = x.shape
    row = lambda b, i: (b, i, 0)
    act = lambda width: pl.BlockSpec((nb, tm, width), row)
    vec = lambda a: pl.BlockSpec(a.shape, lambda b, i: (0, 0))
    resident = lambda a: pl.BlockSpec(a.shape, lambda b, i: (0,) * a.ndim, pipeline_mode=pl.Buffered(1))
    return pl.pallas_call(
        _out_ffn_kernel,
        grid=(bsz // nb, t // tm),
        in_specs=[act(d), act(D_A), act(D_C),
                  resident(wo2), vec(g2), resident(wg), resident(wu), resident(wd), vec(gf)],
        out_specs=act(d),
        out_shape=jax.ShapeDtypeStruct((bsz, t, d), F32),
        scratch_shapes=[pltpu.VMEM((nb * tm, d), F32), pltpu.VMEM((nb * tm, d), BF16)]
        + [pltpu.VMEM((2, nb * tm, 2 * FF_CHUNK), F32)] * 2,
        compiler_params=pltpu.CompilerParams(
            dimension_semantics=("arbitrary", "arbitrary"), vmem_limit_bytes=VMEM_LIMIT),
        name="out_ffn",
    )(x, o, c, wo2, g2, wg, wu, wd, gf)


def _row(a):
    return a.reshape(1, -1)


def kernel(x_prompt, x_sample, cache_k, cache_v, state_conv, w_in, sb_norm_g, conv_w, conv_b,
           conv_ln_g, conv_ln_b, w_out, norm1_g, norm2_g, w_gate, w_up, w_down, final_g):
    assert w_in.shape[0] == 1, "single-layer step"
    d = x_prompt.shape[-1]

    w = w_in[0].astype(BF16)
    wkvt = w[:, D_A:3 * D_A].T.reshape(2, D_A, d)
    wo2 = w_out[0].astype(BF16).reshape(2, D_A, d)
    wg, wu, wd = w_gate[0].astype(BF16), w_up[0].astype(BF16), w_down[0].astype(BF16)
    g1, g2, gf = _row(norm1_g[0]), _row(norm2_g[0]), _row(final_g)
    sbg = _row(sb_norm_g[0])
    cw, cb = conv_w[0], _row(conv_b[0])
    lg, lb = _row(conv_ln_g[0]), _row(conv_ln_b[0])
    idx = jnp.arange(KEY_BLOCK)
    tri = (idx[:, None] >= idx[None, :]).astype(BF16)
    lane = jnp.arange(LANES) // HEAD_DIM
    ones = ((lane[:, None] == lane[None, :]) * (1.0 / HEAD_DIM)).astype(BF16)
    conv_params = (cw, cb, lg, lb)
    tail = (wo2, g2, wg, wu, wd, gf)

    q, ktb, vtb, ktf, vtf, c, u_tail = _in_proj_prompt(x_prompt, g1, w, wkvt, *conv_params)
    o = _sb_prompt(q, ktb, vtb, tri, ones, sbg)
    y_prompt = _out_ffn(x_prompt, o, c, *tail, 1, FFN_TILE)
    k_prompt = jnp.swapaxes(ktf, 3, 4)
    v_prompt = jnp.swapaxes(vtf, 3, 4)
    conv_prompt = u_tail[None, :, HALO - CONV_HIST:, :]

    bs, ts, _ = x_sample.shape
    hist = jnp.pad(state_conv[0], ((0, 0), (HALO - CONV_HIST, 0), (0, 0)))
    qs, kbs, vbs, k_sample, v_sample, cs, us = _in_proj_sample(x_sample, g1, w, hist, *conv_params)
    os_ = _sb_sample(qs, kbs, vbs, jnp.swapaxes(cache_k, 3, 4), jnp.swapaxes(cache_v, 3, 4), tri, ones, sbg)
    y_sample = _out_ffn(x_sample, os_, cs, *tail, bs, ts)
    conv_sample = jnp.concatenate([state_conv[0], us], axis=1)[None, :, -CONV_HIST:, :]

    return (y_prompt, y_sample, k_prompt, v_prompt, conv_prompt, k_sample, v_sample, conv_sample)
```

```python
import jax
import jax.numpy as jnp
from jax import lax
from jax.experimental import pallas as pl
from jax.experimental.pallas import tpu as pltpu

N_HEADS = 8
HEAD_DIM = 64
D_A = N_HEADS * HEAD_DIM
D_C = 512
CONV_W = 31
CONV_HIST = CONV_W - 1
EPS = 1e-6

SUBLANES = 8
LANES = 128
PAIRS = D_A // LANES
HALO = 32
KEY_BLOCK = 256
SAMPLE_HEADS = 8
FF_CHUNK = 256
ROW_TILE = 512
FFN_TILE = 1024
VMEM_LIMIT = 60000 * 1024

LOG2E = 1.4426950408889634
Q_SCALE = HEAD_DIM ** -0.5 * LOG2E
MASKED = -1e30
UNDERFLOW_BITS = 150.0
SOFTPLUS_SATURATION = 100.0

BF16 = jnp.bfloat16
F32 = jnp.float32

NT = (((1,), (1,)), ((), ()))

Q_COLS = slice(0, D_A)
KV_COLS = (slice(D_A, 2 * D_A), slice(2 * D_A, 3 * D_A))
A_COLS = slice(3 * D_A, 3 * D_A + D_C)
G_COLS = slice(3 * D_A + D_C, 3 * D_A + 2 * D_C)


def _rms_scale(x):
    return lax.rsqrt(jnp.mean(x * x, axis=-1, keepdims=True) + EPS)


def _sigmoid(x):
    return 0.5 + 0.5 * jnp.tanh(0.5 * x)


def _causal_dwconv(ext, cw_ref, cb_ref, tm):
    nb, rows, _ = ext.shape
    conv = jnp.zeros((nb, tm, D_C), F32) + cb_ref[...]
    first = HALO - CONV_HIST
    for r in range(SUBLANES):
        offsets = [m for m in range(first, first + CONV_W) if m % SUBLANES == r]
        shifted = ext if r == 0 else pltpu.roll(ext, rows - r, 1)
        for m in offsets:
            tap = m - first
            conv = conv + shifted[:, m - r:m - r + tm, :] * cw_ref[tap:tap + 1, :]
    return conv


def _conv_act(conv, lg_ref, lb_ref):
    mu = jnp.mean(conv, axis=-1, keepdims=True)
    cc = conv - mu
    ln = cc * lax.rsqrt(jnp.mean(cc * cc, axis=-1, keepdims=True) + EPS) * lg_ref[...] + lb_ref[...]
    return (ln * _sigmoid(ln)).astype(BF16)


def _in_proj_prompt_kernel(x_ref, g1_ref, w_ref, wkvt_ref, cw_ref, cb_ref, lg_ref, lb_ref,
                           q_ref, kb_ref, vb_ref, kf_ref, vf_ref, c_ref, tail_ref, hist_ref):
    tm = x_ref.shape[1]

    @pl.when(pl.program_id(1) == 0)
    def _():
        hist_ref[...] = jnp.zeros_like(hist_ref)

    x = x_ref[0]
    h = (x * _rms_scale(x) * g1_ref[...]).astype(BF16)
    a = jnp.dot(h, w_ref[:, A_COLS], preferred_element_type=F32)
    g = jnp.dot(h, w_ref[:, G_COLS], preferred_element_type=F32)
    u = a * _sigmoid(g)
    ext = jnp.concatenate([hist_ref[...], u], axis=0)
    c_ref[0] = _conv_act(_causal_dwconv(ext[None], cw_ref, cb_ref, tm)[0], lg_ref, lb_ref)
    hist_ref[...] = u[tm - HALO:, :]
    tail_ref[0] = u[tm - HALO:, :]

    q_ref[0] = (jnp.dot(h, w_ref[:, Q_COLS], preferred_element_type=F32) * Q_SCALE).astype(BF16)
    for c, b_ref, f_ref in ((0, kb_ref, kf_ref), (1, vb_ref, vf_ref)):
        pt = lax.dot_general(wkvt_ref[c], h, NT, preferred_element_type=F32)
        f_ref[0, 0] = pt.reshape(N_HEADS, HEAD_DIM, tm)
        for j in range(tm // KEY_BLOCK):
            b_ref[0, j] = pt[:, j * KEY_BLOCK:(j + 1) * KEY_BLOCK].astype(BF16)


def _in_proj_prompt(x, g1, w, wkvt, cw, cb, lg, lb):
    bsz, t, d = x.shape
    tm = ROW_TILE
    per_tile = tm // KEY_BLOCK
    row = lambda b, i: (b, i, 0)
    const2 = lambda b, i: (0, 0)
    const3 = lambda b, i: (0, 0, 0)
    act = lambda width: pl.BlockSpec((1, tm, width), row)
    vec = lambda a: pl.BlockSpec(a.shape, const2)
    kv_f = pl.BlockSpec((1, 1, N_HEADS, HEAD_DIM, tm), lambda b, i: (0, b, 0, 0, i))
    kv_b = pl.BlockSpec((1, per_tile, D_A, KEY_BLOCK), lambda b, i: (b, i, 0, 0))
    resident = lambda a: pl.BlockSpec(a.shape, const3, pipeline_mode=pl.Buffered(1))
    return pl.pallas_call(
        _in_proj_prompt_kernel,
        grid=(bsz, t // tm),
        in_specs=[act(d), vec(g1), pl.BlockSpec(w.shape, const2, pipeline_mode=pl.Buffered(1)),
                  resident(wkvt), vec(cw), vec(cb), vec(lg), vec(lb)],
        out_specs=[act(D_A), kv_b, kv_b, kv_f, kv_f, act(D_C),
                   pl.BlockSpec((1, HALO, D_C), lambda b, i: (b, 0, 0))],
        out_shape=[jax.ShapeDtypeStruct((bsz, t, D_A), BF16)]
        + [jax.ShapeDtypeStruct((bsz, t // KEY_BLOCK, D_A, KEY_BLOCK), BF16)] * 2
        + [jax.ShapeDtypeStruct((1, bsz, N_HEADS, HEAD_DIM, t), F32)] * 2
        + [jax.ShapeDtypeStruct((bsz, t, D_C), BF16), jax.ShapeDtypeStruct((bsz, HALO, D_C), F32)],
        scratch_shapes=[pltpu.VMEM((HALO, D_C), F32)],
        compiler_params=pltpu.CompilerParams(
            dimension_semantics=("arbitrary", "arbitrary"), vmem_limit_bytes=VMEM_LIMIT),
        name="in_proj_prompt",
    )(x, g1, w, wkvt, cw, cb, lg, lb)


def _in_proj_sample_kernel(x_ref, g1_ref, w_ref, hist_ref, cw_ref, cb_ref, lg_ref, lb_ref,
                           q_ref, kb_ref, vb_ref, kf_ref, vf_ref, c_ref, u_ref):
    nb, ts, d = x_ref.shape
    m = nb * ts
    x = x_ref[...].reshape(m, d)
    h = (x * _rms_scale(x) * g1_ref[...]).astype(BF16)
    q_ref[...] = (jnp.dot(h, w_ref[:, Q_COLS], preferred_element_type=F32) * Q_SCALE
                  ).astype(BF16).reshape(nb, ts, D_A)
    for c, b_ref, f_ref in ((0, kb_ref, kf_ref), (1, vb_ref, vf_ref)):
        p = jnp.dot(h, w_ref[:, KV_COLS[c]], preferred_element_type=F32)
        b_ref[...] = p.astype(BF16).reshape(nb, ts, D_A)
        for hd in range(N_HEADS):
            f_ref[0, :, hd, :, :] = p[:, hd * HEAD_DIM:(hd + 1) * HEAD_DIM].reshape(nb, ts, HEAD_DIM)
    a = jnp.dot(h, w_ref[:, A_COLS], preferred_element_type=F32)
    g = jnp.dot(h, w_ref[:, G_COLS], preferred_element_type=F32)
    u = (a * _sigmoid(g)).reshape(nb, ts, D_C)
    u_ref[...] = u
    ext = jnp.concatenate([hist_ref[...], u], axis=1)
    conv = _causal_dwconv(ext, cw_ref, cb_ref, ts).reshape(m, D_C)
    c_ref[...] = _conv_act(conv, lg_ref, lb_ref).reshape(nb, ts, D_C)


def _in_proj_sample(x, g1, w, hist, cw, cb, lg, lb):
    bsz, ts, d = x.shape
    args = (x, g1, w, hist, cw, cb, lg, lb)
    full = lambda shape: pl.BlockSpec(shape, lambda i: (0,) * len(shape))
    act = (bsz, ts, D_A)
    kvf = (1, bsz, N_HEADS, ts, HEAD_DIM)
    return pl.pallas_call(
        _in_proj_sample_kernel,
        grid=(1,),
        in_specs=[full(a.shape) for a in args],
        out_specs=[full(act), full(act), full(act), full(kvf), full(kvf), full(act), full(act)],
        out_shape=[jax.ShapeDtypeStruct(act, BF16)] * 3
        + [jax.ShapeDtypeStruct(kvf, F32)] * 2
        + [jax.ShapeDtypeStruct((bsz, ts, D_C), BF16), jax.ShapeDtypeStruct((bsz, ts, D_C), F32)],
        compiler_params=pltpu.CompilerParams(
            dimension_semantics=("arbitrary",), vmem_limit_bytes=VMEM_LIMIT),
        name="in_proj_sample",
    )(*args)


def _softplus2(z):
    return jnp.maximum(jnp.log(1.0 + jnp.exp2(jnp.minimum(z, SOFTPLUS_SATURATION))) * LOG2E, z)


def _split_heads(x2):
    lane = lax.broadcasted_iota(jnp.int32, x2.shape, 1)
    zero = jnp.zeros_like(x2)
    return jnp.where(lane < HEAD_DIM, x2, zero), jnp.where(lane < HEAD_DIM, zero, x2)


def _head_norm(o, g, ones):
    sq = o * o
    hi = sq.astype(BF16)
    lo = (sq - hi.astype(F32)).astype(BF16)
    ms = jnp.dot(hi, ones, preferred_element_type=F32) + jnp.dot(lo, ones, preferred_element_type=F32)
    return o * lax.rsqrt(ms + EPS) * g


def _sb_prompt_kernel(q_ref, kt_ref, vt_ref, tri_ref, ones_ref, g_ref, o_ref, acc_ref, carry_ref):
    i = pl.program_id(1)
    tq = q_ref.shape[1]
    half = tq // 2
    tri = tri_ref[...]
    qs = [_split_heads(q_ref[0, :, p * LANES:(p + 1) * LANES]) for p in range(PAIRS)]
    whole = slice(0, KEY_BLOCK)

    def scores(q_rows, kb, keys):
        return [jnp.dot(qs[p][hh][q_rows], kt_ref[0, kb, p * LANES:(p + 1) * LANES, keys],
                        preferred_element_type=F32) for p in range(PAIRS) for hh in range(2)]

    def causal(z, first_row):
        r = lax.broadcasted_iota(jnp.int32, z.shape, 0) + first_row
        c = lax.broadcasted_iota(jnp.int32, z.shape, 1)
        return jnp.where(c < r, z, MASKED)

    def finish(zs, carries, kb, keys):
        span = keys.stop - keys.start
        sps = [_softplus2(z).astype(BF16) for z in zs]
        locs = [jnp.dot(sp, tri[:span, :span], preferred_element_type=F32) for sp in sps]
        ws, new_carries = [], []
        for z, loc, carry in zip(zs, locs, carries):
            total = loc if carry is None else loc + carry
            ws.append(jnp.exp2(z - total).astype(BF16))
            new_carries.append(loc[:, :1] if carry is None else carry + loc[:, :1])
        first = lax.broadcasted_iota(jnp.int32, (zs[0].shape[0], LANES), 1) < HEAD_DIM
        pairs = []
        for p in range(PAIRS):
            vt2 = vt_ref[0, kb, p * LANES:(p + 1) * LANES, keys]
            pvs = [lax.dot_general(ws[2 * p + hh], vt2, NT, preferred_element_type=F32)
                   for hh in range(2)]
            pairs.append(jnp.where(first, pvs[0], pvs[1]))
        return pairs, new_carries

    def lowest_of(carries):
        low = carries[0]
        for c in carries[1:]:
            low = jnp.minimum(low, c)
        return jnp.min(low)

    def first_visit(with_previous):
        top, bottom, old_keys = slice(0, half), slice(half, tq), slice(0, half)
        z_bot = [causal(z, half) for z in scores(bottom, i, whole)]
        z_top = [causal(z, 0) for z in scores(top, i, old_keys)]
        if with_previous:
            z_prev = scores(slice(0, tq), i - 1, whole)
        pairs_b, carry_b = finish(z_bot, [None] * N_HEADS, i, whole)
        pairs_t, carry_t = finish(z_top, [None] * N_HEADS, i, old_keys)
        for hd in range(N_HEADS):
            carry_ref[hd, :half] = carry_t[hd]
            carry_ref[hd, half:] = carry_b[hd]
        for p in range(PAIRS):
            acc_ref[p, :half] = pairs_t[p]
            acc_ref[p, half:] = pairs_b[p]
        carries = [carry_ref[hd] for hd in range(N_HEADS)]
        if with_previous:
            pairs, carries = finish(z_prev, carries, i - 1, whole)
            for hd in range(N_HEADS):
                carry_ref[hd] = carries[hd]
            for p in range(PAIRS):
                acc_ref[p] += pairs[p]
        return lowest_of(carries)

    def visit(kb):
        carries = [carry_ref[hd] for hd in range(N_HEADS)]
        pairs, carries = finish(scores(slice(0, tq), kb, whole), carries, kb, whole)
        for hd in range(N_HEADS):
            carry_ref[hd] = carries[hd]
        for p in range(PAIRS):
            acc_ref[p] += pairs[p]
        return lowest_of(carries)

    lowest = lax.cond(i == 0, lambda: first_visit(False), lambda: first_visit(True))

    def more(state):
        j, lowest = state
        return jnp.logical_and(j < i, lowest < UNDERFLOW_BITS)

    def body(state):
        j, _ = state
        return j + 1, visit(i - 1 - j)

    lax.while_loop(more, body, (jnp.int32(1), lowest))

    for p in range(PAIRS):
        sl = slice(p * LANES, (p + 1) * LANES)
        o_ref[0, :, sl] = _head_norm(acc_ref[p], g_ref[:, sl], ones_ref[...]).astype(BF16)


def _sb_prompt(q, kt, vt, tri, ones, g):
    bsz, t, _ = q.shape
    tq = KEY_BLOCK
    q_spec = pl.BlockSpec((1, tq, D_A), lambda b, i: (b, i, 0))
    kv_spec = pl.BlockSpec((1,) + kt.shape[1:], lambda b, i: (b, 0, 0, 0))
    return pl.pallas_call(
        _sb_prompt_kernel,
        grid=(bsz, t // tq),
        in_specs=[q_spec, kv_spec, kv_spec,
                  pl.BlockSpec(tri.shape, lambda b, i: (0, 0)),
                  pl.BlockSpec(ones.shape, lambda b, i: (0, 0)),
                  pl.BlockSpec((1, D_A), lambda b, i: (0, 0))],
        out_specs=q_spec,
        out_shape=jax.ShapeDtypeStruct((bsz, t, D_A), BF16),
        scratch_shapes=[pltpu.VMEM((PAIRS, tq, LANES), F32), pltpu.VMEM((N_HEADS, tq, 1), F32)],
        compiler_params=pltpu.CompilerParams(
            dimension_semantics=("arbitrary", "arbitrary"), vmem_limit_bytes=VMEM_LIMIT),
        name="sb_prompt",
    )(q, kt, vt, tri, ones, g)


def _sb_sample_kernel(q_ref, kn_ref, vn_ref, ckt_ref, cvt_ref, tri_ref, ones_ref, g_ref, o_ref):
    ts = q_ref.shape[1]
    past = ckt_ref.shape[4]
    nblk = past // KEY_BLOCK
    tri = tri_ref[...]
    row = lax.broadcasted_iota(jnp.int32, (ts, ts), 0)
    col = lax.broadcasted_iota(jnp.int32, (ts, ts), 1)
    causal = col < row
    heads = range(SAMPLE_HEADS)
    cols = [slice(hd * HEAD_DIM, (hd + 1) * HEAD_DIM) for hd in heads]
    qs = [q_ref[0][:, sl] for sl in cols]

    z_c = [jnp.dot(qs[hd], ckt_ref[0, 0, hd].astype(BF16), preferred_element_type=F32) for hd in heads]
    z_n = [jnp.where(causal, lax.dot_general(qs[hd], kn_ref[0][:, cols[hd]], NT,
                                             preferred_element_type=F32), MASKED) for hd in heads]
    c_n = [jnp.dot(_softplus2(z).astype(BF16), tri[:ts, :ts], preferred_element_type=F32) for z in z_n]
    sps = [_softplus2(z).astype(BF16) for z in z_c]
    locs = [jnp.dot(jnp.concatenate([sp[:, b * KEY_BLOCK:(b + 1) * KEY_BLOCK] for b in range(nblk)],
                                    axis=0), tri, preferred_element_type=F32) for sp in sps]
    outs = []
    for hd in heads:
        carry = c_n[hd][:, :1]
        cs = [None] * nblk
        for b in reversed(range(nblk)):
            loc = locs[hd][b * ts:(b + 1) * ts, :]
            cs[b] = loc + carry
            carry = carry + loc[:, :1]
        w_c = jnp.exp2(z_c[hd] - jnp.concatenate(cs, axis=1)).astype(BF16)
        w_n = jnp.exp2(z_n[hd] - c_n[hd]).astype(BF16)
        outs.append(jnp.dot(w_n, vn_ref[0][:, cols[hd]], preferred_element_type=F32)
                    + lax.dot_general(w_c, cvt_ref[0, 0, hd].astype(BF16), NT,
                                      preferred_element_type=F32))
    for p in range(SAMPLE_HEADS // 2):
        sl = slice(p * LANES, (p + 1) * LANES)
        o = jnp.concatenate(outs[2 * p:2 * p + 2], axis=-1)
        o_ref[0, :, sl] = _head_norm(o, g_ref[:, sl], ones_ref[...]).astype(BF16)


def _sb_sample(q, kn, vn, cache_kt, cache_vt, tri, ones, g):
    bsz, ts, _ = q.shape
    past = cache_kt.shape[4]
    width = SAMPLE_HEADS * HEAD_DIM
    new = pl.BlockSpec((1, ts, width), lambda b, p: (b, 0, p))
    cache = pl.BlockSpec((1, 1, SAMPLE_HEADS, HEAD_DIM, past), lambda b, p: (0, b, p, 0, 0))
    return pl.pallas_call(
        _sb_sample_kernel,
        grid=(bsz, N_HEADS // SAMPLE_HEADS),
        in_specs=[new, new, new, cache, cache,
                  pl.BlockSpec(tri.shape, lambda b, p: (0, 0)),
                  pl.BlockSpec(ones.shape, lambda b, p: (0, 0)),
                  pl.BlockSpec((1, width), lambda b, p: (0, p))],
        out_specs=new,
        out_shape=jax.ShapeDtypeStruct((bsz, ts, D_A), BF16),
        compiler_params=pltpu.CompilerParams(
            dimension_semantics=("arbitrary", "arbitrary"), vmem_limit_bytes=VMEM_LIMIT),
        name="sb_sample",
    )(q, kn, vn, cache_kt, cache_vt, tri, ones, g)


def _out_ffn_kernel(x_ref, o_ref, c_ref, wo_ref, g2_ref, wg_ref, wu_ref, wd_ref, gf_ref, y_ref,
                    acc_ref, hf_ref, ga_ref, gb_ref):
    nb, tm, d = x_ref.shape
    m = nb * tm
    n_chunks = wg_ref.shape[1] // FF_CHUNK
    assert n_chunks % 4 == 3 and n_chunks * FF_CHUNK == wg_ref.shape[1]
    mix = jnp.dot(o_ref[...].reshape(m, D_A), wo_ref[0], preferred_element_type=F32)
    mix = mix + jnp.dot(c_ref[...].reshape(m, D_C), wo_ref[1], preferred_element_type=F32)
    x1 = x_ref[...].reshape(m, d) + mix
    hf_ref[...] = (x1 * _rms_scale(x1) * g2_ref[...]).astype(BF16)
    acc_ref[...] = x1

    big = 2 * FF_CHUNK
    n_big = wg_ref.shape[1] // big
    rest = pl.ds(n_big * big, FF_CHUNK)

    def chunk(c):
        return pl.ds(pl.multiple_of(c * big, big), big)

    def gate_up(cols, gu_ref, width):
        hf = hf_ref[...]
        gu_ref[0, :, :width] = jnp.dot(hf, wg_ref[:, cols], preferred_element_type=F32)
        gu_ref[1, :, :width] = jnp.dot(hf, wu_ref[:, cols], preferred_element_type=F32)

    def down(rows, gu_ref, width):
        gate = gu_ref[0, :, :width]
        act = (gate * _sigmoid(gate) * gu_ref[1, :, :width]).astype(BF16)
        acc_ref[...] += jnp.dot(act, wd_ref[rows, :], preferred_element_type=F32)

    gate_up(chunk(0), ga_ref, big)

    def pair(j, _):
        c = 2 * j + 1
        down(chunk(c - 1), ga_ref, big)
        gate_up(chunk(c), gb_ref, big)
        down(chunk(c), gb_ref, big)
        gate_up(chunk(c + 1), ga_ref, big)
        return 0

    lax.fori_loop(0, (n_big - 1) // 2, pair, 0)
    gate_up(rest, gb_ref, FF_CHUNK)
    down(chunk(n_big - 1), ga_ref, big)
    down(rest, gb_ref, FF_CHUNK)
    x2 = acc_ref[...]
    y_ref[...] = (x2 * _rms_scale(x2) * gf_ref[...]).reshape(nb, tm, d)


def _out_ffn(x, o, c, wo2, g2, wg, wu, wd, gf, nb, tm):
    bsz, t, d = x.shape
    row = lambda b, i: (b, i, 0)
    act = lambda width: pl.BlockSpec((nb, tm, width), row)
    vec = lambda a: pl.BlockSpec(a.shape, lambda b, i: (0, 0))
    resident = lambda a: pl.BlockSpec(a.shape, lambda b, i: (0,) * a.ndim, pipeline_mode=pl.Buffered(1))
    return pl.pallas_call(
        _out_ffn_kernel,
        grid=(bsz // nb, t // tm),
        in_specs=[act(d), act(D_A), act(D_C),
                  resident(wo2), vec(g2), resident(wg), resident(wu), resident(wd), vec(gf)],
        out_specs=act(d),
        out_shape=jax.ShapeDtypeStruct((bsz, t, d), F32),
        scratch_shapes=[pltpu.VMEM((nb * tm, d), F32), pltpu.VMEM((nb * tm, d), BF16)]
        + [pltpu.VMEM((2, nb * tm, 2 * FF_CHUNK), F32)] * 2,
        compiler_params=pltpu.CompilerParams(
            dimension_semantics=("arbitrary", "arbitrary"), vmem_limit_bytes=VMEM_LIMIT),
        name="out_ffn",
    )(x, o, c, wo2, g2, wg, wu, wd, gf)


def _row(a):
    return a.reshape(1, -1)


def kernel(x_prompt, x_sample, cache_k, cache_v, state_conv, w_in, sb_norm_g, conv_w, conv_b,
           conv_ln_g, conv_ln_b, w_out, norm1_g, norm2_g, w_gate, w_up, w_down, final_g):
    assert w_in.shape[0] == 1, "single-layer step"
    d = x_prompt.shape[-1]

    w = w_in[0].astype(BF16)
    wkvt = w[:, D_A:3 * D_A].T.reshape(2, D_A, d)
    wo2 = w_out[0].astype(BF16).reshape(2, D_A, d)
    wg, wu, wd = w_gate[0].astype(BF16), w_up[0].astype(BF16), w_down[0].astype(BF16)
    g1, g2, gf = _row(norm1_g[0]), _row(norm2_g[0]), _row(final_g)
    sbg = _row(sb_norm_g[0])
    cw, cb = conv_w[0], _row(conv_b[0])
    lg, lb = _row(conv_ln_g[0]), _row(conv_ln_b[0])
    idx = jnp.arange(KEY_BLOCK)
    tri = (idx[:, None] >= idx[None, :]).astype(BF16)
    lane = jnp.arange(LANES) // HEAD_DIM
    ones = ((lane[:, None] == lane[None, :]) * (1.0 / HEAD_DIM)).astype(BF16)
    conv_params = (cw, cb, lg, lb)
    tail = (wo2, g2, wg, wu, wd, gf)

    q, ktb, vtb, ktf, vtf, c, u_tail = _in_proj_prompt(x_prompt, g1, w, wkvt, *conv_params)
    o = _sb_prompt(q, ktb, vtb, tri, ones, sbg)
    y_prompt = _out_ffn(x_prompt, o, c, *tail, 1, FFN_TILE)
    k_prompt = jnp.swapaxes(ktf, 3, 4)
    v_prompt = jnp.swapaxes(vtf, 3, 4)
    conv_prompt = u_tail[None, :, HALO - CONV_HIST:, :]

    bs, ts, _ = x_sample.shape
    hist = jnp.pad(state_conv[0], ((0, 0), (HALO - CONV_HIST, 0), (0, 0)))
    qs, kbs, vbs, k_sample, v_sample, cs, us = _in_proj_sample(x_sample, g1, w, hist, *conv_params)
    os_ = _sb_sample(qs, kbs, vbs, jnp.swapaxes(cache_k, 3, 4), jnp.swapaxes(cache_v, 3, 4), tri, ones, sbg)
    y_sample = _out_ffn(x_sample, os_, cs, *tail, bs, ts)
    conv_sample = jnp.concatenate([state_conv[0], us], axis=1)[None, :, -CONV_HIST:, :]

    return (y_prompt, y_sample, k_prompt, v_prompt, conv_prompt, k_sample, v_sample, conv_sample)
```

```python
import jax
import jax.numpy as jnp
from jax import lax
from jax.experimental import pallas as pl
from jax.experimental.pallas import tpu as pltpu

N_HEADS = 8
HEAD_DIM = 64
D_A = N_HEADS * HEAD_DIM
D_C = 512
CONV_W = 31
CONV_HIST = CONV_W - 1
EPS = 1e-6

SUBLANES = 8
LANES = 128
PAIRS = D_A // LANES
HALO = 32
KEY_BLOCK = 256
SAMPLE_HEADS = 8
FF_CHUNK = 256
ROW_TILE = 512
FFN_TILE = 1024
VMEM_LIMIT = 60000 * 1024

LOG2E = 1.4426950408889634
Q_SCALE = HEAD_DIM ** -0.5 * LOG2E
MASKED = -1e30
UNDERFLOW_BITS = 150.0
SOFTPLUS_SATURATION = 100.0

BF16 = jnp.bfloat16
F32 = jnp.float32

NT = (((1,), (1,)), ((), ()))

Q_COLS = slice(0, D_A)
KV_COLS = (slice(D_A, 2 * D_A), slice(2 * D_A, 3 * D_A))
A_COLS = slice(3 * D_A, 3 * D_A + D_C)
G_COLS = slice(3 * D_A + D_C, 3 * D_A + 2 * D_C)


def _rms_scale(x):
    return lax.rsqrt(jnp.mean(x * x, axis=-1, keepdims=True) + EPS)


def _sigmoid(x):
    return 0.5 + 0.5 * jnp.tanh(0.5 * x)


def _swish(x):
    h = 0.5 * x
    return h + h * jnp.tanh(h)


def _causal_dwconv(ext, cw_ref, cb_ref, tm):
    nb, rows, _ = ext.shape
    conv = jnp.zeros((nb, tm, D_C), F32) + cb_ref[...]
    first = HALO - CONV_HIST
    for r in range(SUBLANES):
        offsets = [m for m in range(first, first + CONV_W) if m % SUBLANES == r]
        shifted = ext if r == 0 else pltpu.roll(ext, rows - r, 1)
        for m in offsets:
            tap = m - first
            conv = conv + shifted[:, m - r:m - r + tm, :] * cw_ref[tap:tap + 1, :]
    return conv


def _conv_act(conv, lg_ref, lb_ref):
    mu = jnp.mean(conv, axis=-1, keepdims=True)
    cc = conv - mu
    ln = cc * lax.rsqrt(jnp.mean(cc * cc, axis=-1, keepdims=True) + EPS) * lg_ref[...] + lb_ref[...]
    return _swish(ln).astype(BF16)


def _in_proj_prompt_kernel(x_ref, g1_ref, w_ref, wkvt_ref, cw_ref, cb_ref, lg_ref, lb_ref,
                           q_ref, kb_ref, vb_ref, kf_ref, vf_ref, c_ref, tail_ref, hist_ref):
    tm = x_ref.shape[1]

    @pl.when(pl.program_id(1) == 0)
    def _():
        hist_ref[...] = jnp.zeros_like(hist_ref)

    x = x_ref[0]
    h = (x * _rms_scale(x) * g1_ref[...]).astype(BF16)
    a = jnp.dot(h, w_ref[:, A_COLS], preferred_element_type=F32)
    g = jnp.dot(h, w_ref[:, G_COLS], preferred_element_type=F32)
    u = a * _sigmoid(g)
    ext = jnp.concatenate([hist_ref[...], u], axis=0)
    c_ref[0] = _conv_act(_causal_dwconv(ext[None], cw_ref, cb_ref, tm)[0], lg_ref, lb_ref)
    hist_ref[...] = u[tm - HALO:, :]
    tail_ref[0] = u[tm - HALO:, :]

    q_ref[0] = (jnp.dot(h, w_ref[:, Q_COLS], preferred_element_type=F32) * Q_SCALE).astype(BF16)
    for c, b_ref, f_ref in ((0, kb_ref, kf_ref), (1, vb_ref, vf_ref)):
        pt = lax.dot_general(wkvt_ref[c], h, NT, preferred_element_type=F32)
        f_ref[0, 0] = pt.reshape(N_HEADS, HEAD_DIM, tm)
        for j in range(tm // KEY_BLOCK):
            b_ref[0, j] = pt[:, j * KEY_BLOCK:(j + 1) * KEY_BLOCK].astype(BF16)


def _in_proj_prompt(x, g1, w, wkvt, cw, cb, lg, lb):
    bsz, t, d = x.shape
    tm = ROW_TILE
    per_tile = tm // KEY_BLOCK
    row = lambda b, i: (b, i, 0)
    const2 = lambda b, i: (0, 0)
    const3 = lambda b, i: (0, 0, 0)
    act = lambda width: pl.BlockSpec((1, tm, width), row)
    vec = lambda a: pl.BlockSpec(a.shape, const2)
    kv_f = pl.BlockSpec((1, 1, N_HEADS, HEAD_DIM, tm), lambda b, i: (0, b, 0, 0, i))
    kv_b = pl.BlockSpec((1, per_tile, D_A, KEY_BLOCK), lambda b, i: (b, i, 0, 0))
    resident = lambda a: pl.BlockSpec(a.shape, const3, pipeline_mode=pl.Buffered(1))
    return pl.pallas_call(
        _in_proj_prompt_kernel,
        grid=(bsz, t // tm),
        in_specs=[act(d), vec(g1), pl.BlockSpec(w.shape, const2, pipeline_mode=pl.Buffered(1)),
                  resident(wkvt), vec(cw), vec(cb), vec(lg), vec(lb)],
        out_specs=[act(D_A), kv_b, kv_b, kv_f, kv_f, act(D_C),
                   pl.BlockSpec((1, HALO, D_C), lambda b, i: (b, 0, 0))],
        out_shape=[jax.ShapeDtypeStruct((bsz, t, D_A), BF16)]
        + [jax.ShapeDtypeStruct((bsz, t // KEY_BLOCK, D_A, KEY_BLOCK), BF16)] * 2
        + [jax.ShapeDtypeStruct((1, bsz, N_HEADS, HEAD_DIM, t), F32)] * 2
        + [jax.ShapeDtypeStruct((bsz, t, D_C), BF16), jax.ShapeDtypeStruct((bsz, HALO, D_C), F32)],
        scratch_shapes=[pltpu.VMEM((HALO, D_C), F32)],
        compiler_params=pltpu.CompilerParams(
            dimension_semantics=("arbitrary", "arbitrary"), vmem_limit_bytes=VMEM_LIMIT),
        name="in_proj_prompt",
    )(x, g1, w, wkvt, cw, cb, lg, lb)


def _in_proj_sample_kernel(x_ref, g1_ref, w_ref, hist_ref, cw_ref, cb_ref, lg_ref, lb_ref,
                           q_ref, kb_ref, vb_ref, kf_ref, vf_ref, c_ref, u_ref):
    nb, ts, d = x_ref.shape
    m = nb * ts
    x = x_ref[...].reshape(m, d)
    h = (x * _rms_scale(x) * g1_ref[...]).astype(BF16)
    q_ref[...] = (jnp.dot(h, w_ref[:, Q_COLS], preferred_element_type=F32) * Q_SCALE
                  ).astype(BF16).reshape(nb, ts, D_A)
    for c, b_ref, f_ref in ((0, kb_ref, kf_ref), (1, vb_ref, vf_ref)):
        p = jnp.dot(h, w_ref[:, KV_COLS[c]], preferred_element_type=F32)
        b_ref[...] = p.astype(BF16).reshape(nb, ts, D_A)
        for hd in range(N_HEADS):
            f_ref[0, :, hd, :, :] = p[:, hd * HEAD_DIM:(hd + 1) * HEAD_DIM].reshape(nb, ts, HEAD_DIM)
    a = jnp.dot(h, w_ref[:, A_COLS], preferred_element_type=F32)
    g = jnp.dot(h, w_ref[:, G_COLS], preferred_element_type=F32)
    u = (a * _sigmoid(g)).reshape(nb, ts, D_C)
    u_ref[...] = u
    ext = jnp.concatenate([hist_ref[...], u], axis=1)
    conv = _causal_dwconv(ext, cw_ref, cb_ref, ts).reshape(m, D_C)
    c_ref[...] = _conv_act(conv, lg_ref, lb_ref).reshape(nb, ts, D_C)


def _in_proj_sample(x, g1, w, hist, cw, cb, lg, lb):
    bsz, ts, d = x.shape
    args = (x, g1, w, hist, cw, cb, lg, lb)
    full = lambda shape: pl.BlockSpec(shape, lambda i: (0,) * len(shape))
    act = (bsz, ts, D_A)
    kvf = (1, bsz, N_HEADS, ts, HEAD_DIM)
    return pl.pallas_call(
        _in_proj_sample_kernel,
        grid=(1,),
        in_specs=[full(a.shape) for a in args],
        out_specs=[full(act), full(act), full(act), full(kvf), full(kvf), full(act), full(act)],
        out_shape=[jax.ShapeDtypeStruct(act, BF16)] * 3
        + [jax.ShapeDtypeStruct(kvf, F32)] * 2
        + [jax.ShapeDtypeStruct((bsz, ts, D_C), BF16), jax.ShapeDtypeStruct((bsz, ts, D_C), F32)],
        compiler_params=pltpu.CompilerParams(
            dimension_semantics=("arbitrary",), vmem_limit_bytes=VMEM_LIMIT),
        name="in_proj_sample",
    )(*args)


def _softplus2(z):
    return jnp.maximum(jnp.log(1.0 + jnp.exp2(jnp.minimum(z, SOFTPLUS_SATURATION))) * LOG2E, z)


def _split_heads(x2):
    lane = lax.broadcasted_iota(jnp.int32, x2.shape, 1)
    zero = jnp.zeros_like(x2)
    return jnp.where(lane < HEAD_DIM, x2, zero), jnp.where(lane < HEAD_DIM, zero, x2)


def _head_norm(o, g, ones):
    sq = o * o
    hi = sq.astype(BF16)
    lo = (sq - hi.astype(F32)).astype(BF16)
    ms = jnp.dot(hi, ones, preferred_element_type=F32) + jnp.dot(lo, ones, preferred_element_type=F32)
    return o * lax.rsqrt(ms + EPS) * g


def _sb_prompt_kernel(q_ref, kt_ref, vt_ref, tri_ref, ones_ref, g_ref, o_ref, acc_ref, carry_ref):
    i = pl.program_id(1)
    tq = q_ref.shape[1]
    tri = tri_ref[...]
    row = lax.broadcasted_iota(jnp.int32, (tq, KEY_BLOCK), 0)
    col = lax.broadcasted_iota(jnp.int32, (tq, KEY_BLOCK), 1)
    diag = col < row
    qs = [_split_heads(q_ref[0, :, p * LANES:(p + 1) * LANES]) for p in range(PAIRS)]

    def visit(blocks, fresh):
        lowest = None
        first = lax.broadcasted_iota(jnp.int32, (tq, LANES), 1) < HEAD_DIM
        for group in range(2):
            pairs_here = range(group * (PAIRS // 2), (group + 1) * (PAIRS // 2))
            heads = [(p, hh) for p in pairs_here for hh in range(2)]
            zs = {}
            for n, (kb, mask) in enumerate(blocks):
                for p, hh in heads:
                    kt2 = kt_ref[0, kb, p * LANES:(p + 1) * LANES, :]
                    z = jnp.dot(qs[p][hh], kt2, preferred_element_type=F32)
                    zs[n, 2 * p + hh] = z if mask is None else jnp.where(mask, z, MASKED)
            sps = {key: _softplus2(z).astype(BF16) for key, z in zs.items()}
            locs = {key: jnp.dot(sp, tri, preferred_element_type=F32) for key, sp in sps.items()}
            ws = {}
            for p, hh in heads:
                hd = 2 * p + hh
                carry = None if fresh else carry_ref[hd]
                for n in range(len(blocks)):
                    total = locs[n, hd] if carry is None else locs[n, hd] + carry
                    ws[n, hd] = jnp.exp2(zs[n, hd] - total).astype(BF16)
                    carry = locs[n, hd][:, :1] if carry is None else carry + locs[n, hd][:, :1]
                carry_ref[hd] = carry
                lowest = carry if lowest is None else jnp.minimum(lowest, carry)
            for p in pairs_here:
                pair = None
                for n, (kb, _) in enumerate(blocks):
                    vt2 = vt_ref[0, kb, p * LANES:(p + 1) * LANES, :]
                    pvs = [lax.dot_general(ws[n, 2 * p + hh], vt2, NT, preferred_element_type=F32)
                           for hh in range(2)]
                    part = jnp.where(first, pvs[0], pvs[1])
                    pair = part if pair is None else pair + part
                if fresh:
                    acc_ref[p] = pair
                else:
                    acc_ref[p] += pair
        return jnp.min(lowest)

    lowest = lax.cond(i == 0,
                      lambda: visit([(i, diag)], True),
                      lambda: visit([(i, diag), (i - 1, None)], True))

    def more(state):
        j, lowest = state
        return jnp.logical_and(j < i, lowest < UNDERFLOW_BITS)

    def body(state):
        j, _ = state
        return j + 1, visit([(i - 1 - j, None)], False)

    lax.while_loop(more, body, (jnp.int32(1), lowest))

    for p in range(PAIRS):
        sl = slice(p * LANES, (p + 1) * LANES)
        o_ref[0, :, sl] = _head_norm(acc_ref[p], g_ref[:, sl], ones_ref[...]).astype(BF16)


def _sb_prompt(q, kt, vt, tri, ones, g):
    bsz, t, _ = q.shape
    tq = KEY_BLOCK
    q_spec = pl.BlockSpec((1, tq, D_A), lambda b, i: (b, i, 0))
    kv_spec = pl.BlockSpec((1,) + kt.shape[1:], lambda b, i: (b, 0, 0, 0))
    return pl.pallas_call(
        _sb_prompt_kernel,
        grid=(bsz, t // tq),
        in_specs=[q_spec, kv_spec, kv_spec,
                  pl.BlockSpec(tri.shape, lambda b, i: (0, 0)),
                  pl.BlockSpec(ones.shape, lambda b, i: (0, 0)),
                  pl.BlockSpec((1, D_A), lambda b, i: (0, 0))],
        out_specs=q_spec,
        out_shape=jax.ShapeDtypeStruct((bsz, t, D_A), BF16),
        scratch_shapes=[pltpu.VMEM((PAIRS, tq, LANES), F32), pltpu.VMEM((N_HEADS, tq, 1), F32)],
        compiler_params=pltpu.CompilerParams(
            dimension_semantics=("arbitrary", "arbitrary"), vmem_limit_bytes=VMEM_LIMIT),
        name="sb_prompt",
    )(q, kt, vt, tri, ones, g)


def _sb_sample_kernel(q_ref, kn_ref, vn_ref, ckt_ref, cvt_ref, tri_ref, ones_ref, g_ref, o_ref):
    ts = q_ref.shape[1]
    past = ckt_ref.shape[4]
    nblk = past // KEY_BLOCK
    tri = tri_ref[...]
    row = lax.broadcasted_iota(jnp.int32, (ts, ts), 0)
    col = lax.broadcasted_iota(jnp.int32, (ts, ts), 1)
    causal = col < row
    heads = range(SAMPLE_HEADS)
    cols = [slice(hd * HEAD_DIM, (hd + 1) * HEAD_DIM) for hd in heads]
    qs = [q_ref[0][:, sl] for sl in cols]

    z_c = [jnp.dot(qs[hd], ckt_ref[0, 0, hd].astype(BF16), preferred_element_type=F32) for hd in heads]
    z_n = [jnp.where(causal, lax.dot_general(qs[hd], kn_ref[0][:, cols[hd]], NT,
                                             preferred_element_type=F32), MASKED) for hd in heads]
    c_n = [jnp.dot(_softplus2(z).astype(BF16), tri[:ts, :ts], preferred_element_type=F32) for z in z_n]
    sps = [_softplus2(z).astype(BF16) for z in z_c]
    locs = [jnp.dot(jnp.concatenate([sp[:, b * KEY_BLOCK:(b + 1) * KEY_BLOCK] for b in range(nblk)],
                                    axis=0), tri, preferred_element_type=F32) for sp in sps]
    outs = []
    for hd in heads:
        carry = c_n[hd][:, :1]
        cs = [None] * nblk
        for b in reversed(range(nblk)):
            loc = locs[hd][b * ts:(b + 1) * ts, :]
            cs[b] = loc + carry
            carry = carry + loc[:, :1]
        w_c = jnp.exp2(z_c[hd] - jnp.concatenate(cs, axis=1)).astype(BF16)
        w_n = jnp.exp2(z_n[hd] - c_n[hd]).astype(BF16)
        outs.append(jnp.dot(w_n, vn_ref[0][:, cols[hd]], preferred_element_type=F32)
                    + lax.dot_general(w_c, cvt_ref[0, 0, hd].astype(BF16), NT,
                                      preferred_element_type=F32))
    for p in range(SAMPLE_HEADS // 2):
        sl = slice(p * LANES, (p + 1) * LANES)
        o = jnp.concatenate(outs[2 * p:2 * p + 2], axis=-1)
        o_ref[0, :, sl] = _head_norm(o, g_ref[:, sl], ones_ref[...]).astype(BF16)


def _sb_sample(q, kn, vn, cache_kt, cache_vt, tri, ones, g):
    bsz, ts, _ = q.shape
    past = cache_kt.shape[4]
    width = SAMPLE_HEADS * HEAD_DIM
    new = pl.BlockSpec((1, ts, width), lambda b, p: (b, 0, p))
    cache = pl.BlockSpec((1, 1, SAMPLE_HEADS, HEAD_DIM, past), lambda b, p: (0, b, p, 0, 0))
    return pl.pallas_call(
        _sb_sample_kernel,
        grid=(bsz, N_HEADS // SAMPLE_HEADS),
        in_specs=[new, new, new, cache, cache,
                  pl.BlockSpec(tri.shape, lambda b, p: (0, 0)),
                  pl.BlockSpec(ones.shape, lambda b, p: (0, 0)),
                  pl.BlockSpec((1, width), lambda b, p: (0, p))],
        out_specs=new,
        out_shape=jax.ShapeDtypeStruct((bsz, ts, D_A), BF16),
        compiler_params=pltpu.CompilerParams(
            dimension_semantics=("arbitrary", "arbitrary"), vmem_limit_bytes=VMEM_LIMIT),
        name="sb_sample",
    )(q, kn, vn, cache_kt, cache_vt, tri, ones, g)


def _out_ffn_kernel(x_ref, o_ref, c_ref, wo_ref, g2_ref, wg_ref, wu_ref, wd_ref, gf_ref, y_ref,
                    acc_ref, hf_ref, ga_ref, gb_ref):
    nb, tm, d = x_ref.shape
    m = nb * tm
    n_chunks = wg_ref.shape[1] // FF_CHUNK
    assert n_chunks % 4 == 3 and n_chunks * FF_CHUNK == wg_ref.shape[1]
    mix = jnp.dot(o_ref[...].reshape(m, D_A), wo_ref[0], preferred_element_type=F32)
    mix = mix + jnp.dot(c_ref[...].reshape(m, D_C), wo_ref[1], preferred_element_type=F32)
    x1 = x_ref[...].reshape(m, d) + mix
    hf_ref[...] = (x1 * _rms_scale(x1) * g2_ref[...]).astype(BF16)
    acc_ref[...] = x1

    big = 2 * FF_CHUNK
    n_big = wg_ref.shape[1] // big
    rest = pl.ds(n_big * big, FF_CHUNK)

    def chunk(c):
        return pl.ds(pl.multiple_of(c * big, big), big)

    def gate_up(cols, gu_ref, width):
        hf = hf_ref[...]
        gu_ref[0, :, :width] = jnp.dot(hf, wg_ref[:, cols], preferred_element_type=F32)
        gu_ref[1, :, :width] = jnp.dot(hf, wu_ref[:, cols], preferred_element_type=F32)

    def down(rows, gu_ref, width):
        gate = gu_ref[0, :, :width]
        act = (_swish(gate) * gu_ref[1, :, :width]).astype(BF16)
        acc_ref[...] += jnp.dot(act, wd_ref[rows, :], preferred_element_type=F32)

    gate_up(chunk(0), ga_ref, big)

    def pair(j, _):
        c = 2 * j + 1
        down(chunk(c - 1), ga_ref, big)
        gate_up(chunk(c), gb_ref, big)
        down(chunk(c), gb_ref, big)
        gate_up(chunk(c + 1), ga_ref, big)
        return 0

    lax.fori_loop(0, (n_big - 1) // 2, pair, 0)
    gate_up(rest, gb_ref, FF_CHUNK)
    down(chunk(n_big - 1), ga_ref, big)
    down(rest, gb_ref, FF_CHUNK)
    x2 = acc_ref[...]
    y_ref[...] = (x2 * _rms_scale(x2) * gf_ref[...]).reshape(nb, tm, d)


def _out_ffn(x, o, c, wo2, g2, wg, wu, wd, gf, nb, tm):
    bsz, t, d = x.shape
    row = lambda b, i: (b, i, 0)
    act = lambda width: pl.BlockSpec((nb, tm, width), row)
    vec = lambda a: pl.BlockSpec(a.shape, lambda b, i: (0, 0))
    resident = lambda a: pl.BlockSpec(a.shape, lambda b, i: (0,) * a.ndim, pipeline_mode=pl.Buffered(1))
    return pl.pallas_call(
        _out_ffn_kernel,
        grid=(bsz // nb, t // tm),
        in_specs=[act(d), act(D_A), act(D_C),
                  resident(wo2), vec(g2), resident(wg), resident(wu), resident(wd), vec(gf)],
        out_specs=act(d),
        out_shape=jax.ShapeDtypeStruct((bsz, t, d), F32),
        scratch_shapes=[pltpu.VMEM((nb * tm, d), F32), pltpu.VMEM((nb * tm, d), BF16)]
        + [pltpu.VMEM((2, nb * tm, 2 * FF_CHUNK), F32)] * 2,
        compiler_params=pltpu.CompilerParams(
            dimension_semantics=("arbitrary", "arbitrary"), vmem_limit_bytes=VMEM_LIMIT),
        name="out_ffn",
    )(x, o, c, wo2, g2, wg, wu, wd, gf)


def _row(a):
    return a.reshape(1, -1)


def kernel(x_prompt, x_sample, cache_k, cache_v, state_conv, w_in, sb_norm_g, conv_w, conv_b,
           conv_ln_g, conv_ln_b, w_out, norm1_g, norm2_g, w_gate, w_up, w_down, final_g):
    assert w_in.shape[0] == 1, "single-layer step"
    d = x_prompt.shape[-1]

    w = w_in[0].astype(BF16)
    wkvt = w[:, D_A:3 * D_A].T.reshape(2, D_A, d)
    wo2 = w_out[0].astype(BF16).reshape(2, D_A, d)
    wg, wu, wd = w_gate[0].astype(BF16), w_up[0].astype(BF16), w_down[0].astype(BF16)
    g1, g2, gf = _row(norm1_g[0]), _row(norm2_g[0]), _row(final_g)
    sbg = _row(sb_norm_g[0])
    cw, cb = conv_w[0], _row(conv_b[0])
    lg, lb = _row(conv_ln_g[0]), _row(conv_ln_b[0])
    idx = jnp.arange(KEY_BLOCK)
    tri = (idx[:, None] >= idx[None, :]).astype(BF16)
    lane = jnp.arange(LANES) // HEAD_DIM
    ones = ((lane[:, None] == lane[None, :]) * (1.0 / HEAD_DIM)).astype(BF16)
    conv_params = (cw, cb, lg, lb)
    tail = (wo2, g2, wg, wu, wd, gf)

    q, ktb, vtb, ktf, vtf, c, u_tail = _in_proj_prompt(x_prompt, g1, w, wkvt, *conv_params)
    o = _sb_prompt(q, ktb, vtb, tri, ones, sbg)
    y_prompt = _out_ffn(x_prompt, o, c, *tail, 1, FFN_TILE)
    k_prompt = jnp.swapaxes(ktf, 3, 4)
    v_prompt = jnp.swapaxes(vtf, 3, 4)
    conv_prompt = u_tail[None, :, HALO - CONV_HIST:, :]

    bs, ts, _ = x_sample.shape
    hist = jnp.pad(state_conv[0], ((0, 0), (HALO - CONV_HIST, 0), (0, 0)))
    qs, kbs, vbs, k_sample, v_sample, cs, us = _in_proj_sample(x_sample, g1, w, hist, *conv_params)
    os_ = _sb_sample(qs, kbs, vbs, jnp.swapaxes(cache_k, 3, 4), jnp.swapaxes(cache_v, 3, 4), tri, ones, sbg)
    y_sample = _out_ffn(x_sample, os_, cs, *tail, bs, ts)
    conv_sample = jnp.concatenate([state_conv[0], us], axis=1)[None, :, -CONV_HIST:, :]

    return (y_prompt, y_sample, k_prompt, v_prompt, conv_prompt, k_sample, v_sample, conv_sample)
```

```python
import jax
import jax.numpy as jnp
from jax import lax
from jax.experimental import pallas as pl
from jax.experimental.pallas import tpu as pltpu

N_HEADS = 8
HEAD_DIM = 64
D_A = N_HEADS * HEAD_DIM
D_C = 512
CONV_W = 31
CONV_HIST = CONV_W - 1
EPS = 1e-6

SUBLANES = 8
LANES = 128
PAIRS = D_A // LANES
HALO = 32
KEY_BLOCK = 256
SAMPLE_HEADS = 8
FF_CHUNK = 256
ROW_TILE = 512
FFN_TILE = 1024
VMEM_LIMIT = 60000 * 1024

LOG2E = 1.4426950408889634
Q_SCALE = HEAD_DIM ** -0.5 * LOG2E
MASKED = -1e30
UNDERFLOW_BITS = 150.0
SOFTPLUS_SATURATION = 100.0

BF16 = jnp.bfloat16
F32 = jnp.float32

NT = (((1,), (1,)), ((), ()))

Q_COLS = slice(0, D_A)
KV_COLS = (slice(D_A, 2 * D_A), slice(2 * D_A, 3 * D_A))
A_COLS = slice(3 * D_A, 3 * D_A + D_C)
G_COLS = slice(3 * D_A + D_C, 3 * D_A + 2 * D_C)


def _rms_scale(x):
    return lax.rsqrt(jnp.mean(x * x, axis=-1, keepdims=True) + EPS)


def _sigmoid(x):
    return 0.5 + 0.5 * jnp.tanh(0.5 * x)


def _swish(x):
    h = 0.5 * x
    return h + h * jnp.tanh(h)


def _causal_dwconv(ext, cw_ref, cb_ref, tm):
    nb, rows, _ = ext.shape
    conv = jnp.zeros((nb, tm, D_C), F32) + cb_ref[...]
    first = HALO - CONV_HIST
    for r in range(SUBLANES):
        offsets = [m for m in range(first, first + CONV_W) if m % SUBLANES == r]
        shifted = ext if r == 0 else pltpu.roll(ext, rows - r, 1)
        for m in offsets:
            tap = m - first
            conv = conv + shifted[:, m - r:m - r + tm, :] * cw_ref[tap:tap + 1, :]
    return conv


def _conv_act(conv, lg_ref, lb_ref):
    mu = jnp.mean(conv, axis=-1, keepdims=True)
    cc = conv - mu
    ln = cc * lax.rsqrt(jnp.mean(cc * cc, axis=-1, keepdims=True) + EPS) * lg_ref[...] + lb_ref[...]
    return _swish(ln).astype(BF16)


def _conv_group_kernel(u_ref, halo_ref, cw_ref, cb_ref, lg_ref, lb_ref, c_ref):
    tm = u_ref.shape[1]
    halo = jnp.where(pl.program_id(1) == 0, 0.0, halo_ref[0])
    ext = jnp.concatenate([halo, u_ref[0]], axis=0)
    c_ref[0] = _conv_act(_causal_dwconv(ext[None], cw_ref, cb_ref, tm)[0], lg_ref, lb_ref)


def _conv_group(u, cw, cb, lg, lb):
    bsz, t, _ = u.shape
    tm = ROW_TILE
    per_tile = tm // HALO
    vec = lambda a: pl.BlockSpec(a.shape, lambda b, i: (0, 0))
    return pl.pallas_call(
        _conv_group_kernel,
        grid=(bsz, t // tm),
        in_specs=[pl.BlockSpec((1, tm, D_C), lambda b, i: (b, i, 0)),
                  pl.BlockSpec((1, HALO, D_C), lambda b, i: (b, jnp.maximum(i * per_tile - 1, 0), 0)),
                  vec(cw), vec(cb), vec(lg), vec(lb)],
        out_specs=pl.BlockSpec((1, tm, D_C), lambda b, i: (b, i, 0)),
        out_shape=jax.ShapeDtypeStruct((bsz, t, D_C), BF16),
        compiler_params=pltpu.CompilerParams(
            dimension_semantics=("arbitrary", "arbitrary"), vmem_limit_bytes=VMEM_LIMIT),
        name="conv_group",
    )(u, u, cw, cb, lg, lb)


def _in_proj_prompt_kernel(x_ref, g1_ref, w_ref, wkvt_ref,
                           q_ref, kb_ref, vb_ref, kf_ref, vf_ref, u_ref):
    tm = x_ref.shape[1]
    x = x_ref[0]
    h = (x * _rms_scale(x) * g1_ref[...]).astype(BF16)
    a = jnp.dot(h, w_ref[:, A_COLS], preferred_element_type=F32)
    g = jnp.dot(h, w_ref[:, G_COLS], preferred_element_type=F32)
    u_ref[0] = a * _sigmoid(g)

    q_ref[0] = (jnp.dot(h, w_ref[:, Q_COLS], preferred_element_type=F32) * Q_SCALE).astype(BF16)
    for c, b_ref, f_ref in ((0, kb_ref, kf_ref), (1, vb_ref, vf_ref)):
        pt = lax.dot_general(wkvt_ref[c], h, NT, preferred_element_type=F32)
        f_ref[0, 0] = pt.reshape(N_HEADS, HEAD_DIM, tm)
        for j in range(tm // KEY_BLOCK):
            b_ref[0, j] = pt[:, j * KEY_BLOCK:(j + 1) * KEY_BLOCK].astype(BF16)


def _in_proj_prompt(x, g1, w, wkvt):
    bsz, t, d = x.shape
    tm = ROW_TILE
    per_tile = tm // KEY_BLOCK
    row = lambda b, i: (b, i, 0)
    const2 = lambda b, i: (0, 0)
    const3 = lambda b, i: (0, 0, 0)
    act = lambda width: pl.BlockSpec((1, tm, width), row)
    vec = lambda a: pl.BlockSpec(a.shape, const2)
    kv_f = pl.BlockSpec((1, 1, N_HEADS, HEAD_DIM, tm), lambda b, i: (0, b, 0, 0, i))
    kv_b = pl.BlockSpec((1, per_tile, D_A, KEY_BLOCK), lambda b, i: (b, i, 0, 0))
    resident = lambda a: pl.BlockSpec(a.shape, const3, pipeline_mode=pl.Buffered(1))
    return pl.pallas_call(
        _in_proj_prompt_kernel,
        grid=(bsz, t // tm),
        in_specs=[act(d), vec(g1), pl.BlockSpec(w.shape, const2, pipeline_mode=pl.Buffered(1)),
                  resident(wkvt)],
        out_specs=[act(D_A), kv_b, kv_b, kv_f, kv_f, act(D_C)],
        out_shape=[jax.ShapeDtypeStruct((bsz, t, D_A), BF16)]
        + [jax.ShapeDtypeStruct((bsz, t // KEY_BLOCK, D_A, KEY_BLOCK), BF16)] * 2
        + [jax.ShapeDtypeStruct((1, bsz, N_HEADS, HEAD_DIM, t), F32)] * 2
        + [jax.ShapeDtypeStruct((bsz, t, D_C), F32)],
        compiler_params=pltpu.CompilerParams(
            dimension_semantics=("arbitrary", "arbitrary"), vmem_limit_bytes=VMEM_LIMIT),
        name="in_proj_prompt",
    )(x, g1, w, wkvt)


def _in_proj_sample_kernel(x_ref, g1_ref, w_ref, hist_ref, cw_ref, cb_ref, lg_ref, lb_ref,
                           q_ref, kb_ref, vb_ref, kf_ref, vf_ref, c_ref, u_ref):
    nb, ts, d = x_ref.shape
    m = nb * ts
    x = x_ref[...].reshape(m, d)
    h = (x * _rms_scale(x) * g1_ref[...]).astype(BF16)
    q_ref[...] = (jnp.dot(h, w_ref[:, Q_COLS], preferred_element_type=F32) * Q_SCALE
                  ).astype(BF16).reshape(nb, ts, D_A)
    for c, b_ref, f_ref in ((0, kb_ref, kf_ref), (1, vb_ref, vf_ref)):
        p = jnp.dot(h, w_ref[:, KV_COLS[c]], preferred_element_type=F32)
        b_ref[...] = p.astype(BF16).reshape(nb, ts, D_A)
        for hd in range(N_HEADS):
            f_ref[0, :, hd, :, :] = p[:, hd * HEAD_DIM:(hd + 1) * HEAD_DIM].reshape(nb, ts, HEAD_DIM)
    a = jnp.dot(h, w_ref[:, A_COLS], preferred_element_type=F32)
    g = jnp.dot(h, w_ref[:, G_COLS], preferred_element_type=F32)
    u = (a * _sigmoid(g)).reshape(nb, ts, D_C)
    u_ref[...] = u
    ext = jnp.concatenate([hist_ref[...], u], axis=1)
    conv = _causal_dwconv(ext, cw_ref, cb_ref, ts).reshape(m, D_C)
    c_ref[...] = _conv_act(conv, lg_ref, lb_ref).reshape(nb, ts, D_C)


def _in_proj_sample(x, g1, w, hist, cw, cb, lg, lb):
    bsz, ts, d = x.shape
    args = (x, g1, w, hist, cw, cb, lg, lb)
    full = lambda shape: pl.BlockSpec(shape, lambda i: (0,) * len(shape))
    act = (bsz, ts, D_A)
    kvf = (1, bsz, N_HEADS, ts, HEAD_DIM)
    return pl.pallas_call(
        _in_proj_sample_kernel,
        grid=(1,),
        in_specs=[full(a.shape) for a in args],
        out_specs=[full(act), full(act), full(act), full(kvf), full(kvf), full(act), full(act)],
        out_shape=[jax.ShapeDtypeStruct(act, BF16)] * 3
        + [jax.ShapeDtypeStruct(kvf, F32)] * 2
        + [jax.ShapeDtypeStruct((bsz, ts, D_C), BF16), jax.ShapeDtypeStruct((bsz, ts, D_C), F32)],
        compiler_params=pltpu.CompilerParams(
            dimension_semantics=("arbitrary",), vmem_limit_bytes=VMEM_LIMIT),
        name="in_proj_sample",
    )(*args)


def _softplus2(z):
    return jnp.maximum(jnp.log(1.0 + jnp.exp2(jnp.minimum(z, SOFTPLUS_SATURATION))) * LOG2E, z)


def _split_heads(x2):
    lane = lax.broadcasted_iota(jnp.int32, x2.shape, 1)
    zero = jnp.zeros_like(x2)
    return jnp.where(lane < HEAD_DIM, x2, zero), jnp.where(lane < HEAD_DIM, zero, x2)


def _head_norm(o, g, ones):
    sq = o * o
    hi = sq.astype(BF16)
    lo = (sq - hi.astype(F32)).astype(BF16)
    ms = jnp.dot(hi, ones, preferred_element_type=F32) + jnp.dot(lo, ones, preferred_element_type=F32)
    return o * lax.rsqrt(ms + EPS) * g


def _sb_prompt_kernel(q_ref, kt_ref, vt_ref, tri_ref, ones_ref, g_ref, o_ref, acc_ref, carry_ref):
    i = pl.program_id(1)
    tq = q_ref.shape[1]
    tri = tri_ref[...]
    row = lax.broadcasted_iota(jnp.int32, (tq, KEY_BLOCK), 0)
    col = lax.broadcasted_iota(jnp.int32, (tq, KEY_BLOCK), 1)
    diag = col < row
    qs = [_split_heads(q_ref[0, :, p * LANES:(p + 1) * LANES]) for p in range(PAIRS)]

    def visit(blocks, fresh):
        lowest = None
        first = lax.broadcasted_iota(jnp.int32, (tq, LANES), 1) < HEAD_DIM
        for group in range(2):
            pairs_here = range(group * (PAIRS // 2), (group + 1) * (PAIRS // 2))
            heads = [(p, hh) for p in pairs_here for hh in range(2)]
            zs = {}
            for n, (kb, mask) in enumerate(blocks):
                for p, hh in heads:
                    kt2 = kt_ref[0, kb, p * LANES:(p + 1) * LANES, :]
                    z = jnp.dot(qs[p][hh], kt2, preferred_element_type=F32)
                    zs[n, 2 * p + hh] = z if mask is None else jnp.where(mask, z, MASKED)
            sps = {key: _softplus2(z).astype(BF16) for key, z in zs.items()}
            locs = {key: jnp.dot(sp, tri, preferred_element_type=F32) for key, sp in sps.items()}
            ws = {}
            for p, hh in heads:
                hd = 2 * p + hh
                carry = None if fresh else carry_ref[hd]
                for n in range(len(blocks)):
                    total = locs[n, hd] if carry is None else locs[n, hd] + carry
                    ws[n, hd] = jnp.exp2(zs[n, hd] - total).astype(BF16)
                    carry = locs[n, hd][:, :1] if carry is None else carry + locs[n, hd][:, :1]
                carry_ref[hd] = carry
                lowest = carry if lowest is None else jnp.minimum(lowest, carry)
            for p in pairs_here:
                pair = None
                for n, (kb, _) in enumerate(blocks):
                    vt2 = vt_ref[0, kb, p * LANES:(p + 1) * LANES, :]
                    pvs = [lax.dot_general(ws[n, 2 * p + hh], vt2, NT, preferred_element_type=F32)
                           for hh in range(2)]
                    part = jnp.where(first, pvs[0], pvs[1])
                    pair = part if pair is None else pair + part
                if fresh:
                    acc_ref[p] = pair
                else:
                    acc_ref[p] += pair
        return jnp.min(lowest)

    lowest = lax.cond(i == 0,
                      lambda: visit([(i, diag)], True),
                      lambda: visit([(i, diag), (i - 1, None)], True))

    def more(state):
        j, lowest = state
        return jnp.logical_and(j < i, lowest < UNDERFLOW_BITS)

    def body(state):
        j, _ = state
        return j + 1, visit([(i - 1 - j, None)], False)

    lax.while_loop(more, body, (jnp.int32(1), lowest))

    for p in range(PAIRS):
        sl = slice(p * LANES, (p + 1) * LANES)
        o_ref[0, :, sl] = _head_norm(acc_ref[p], g_ref[:, sl], ones_ref[...]).astype(BF16)


def _sb_prompt(q, kt, vt, tri, ones, g):
    bsz, t, _ = q.shape
    tq = KEY_BLOCK
    q_spec = pl.BlockSpec((1, tq, D_A), lambda b, i: (b, i, 0))
    kv_spec = pl.BlockSpec((1,) + kt.shape[1:], lambda b, i: (b, 0, 0, 0))
    return pl.pallas_call(
        _sb_prompt_kernel,
        grid=(bsz, t // tq),
        in_specs=[q_spec, kv_spec, kv_spec,
                  pl.BlockSpec(tri.shape, lambda b, i: (0, 0)),
                  pl.BlockSpec(ones.shape, lambda b, i: (0, 0)),
                  pl.BlockSpec((1, D_A), lambda b, i: (0, 0))],
        out_specs=q_spec,
        out_shape=jax.ShapeDtypeStruct((bsz, t, D_A), BF16),
        scratch_shapes=[pltpu.VMEM((PAIRS, tq, LANES), F32), pltpu.VMEM((N_HEADS, tq, 1), F32)],
        compiler_params=pltpu.CompilerParams(
            dimension_semantics=("arbitrary", "arbitrary"), vmem_limit_bytes=VMEM_LIMIT),
        name="sb_prompt",
    )(q, kt, vt, tri, ones, g)


def _sb_sample_kernel(q_ref, kn_ref, vn_ref, ckt_ref, cvt_ref, tri_ref, ones_ref, g_ref, o_ref):
    ts = q_ref.shape[1]
    past = ckt_ref.shape[4]
    nblk = past // KEY_BLOCK
    tri = tri_ref[...]
    row = lax.broadcasted_iota(jnp.int32, (ts, ts), 0)
    col = lax.broadcasted_iota(jnp.int32, (ts, ts), 1)
    causal = col < row
    heads = range(SAMPLE_HEADS)
    cols = [slice(hd * HEAD_DIM, (hd + 1) * HEAD_DIM) for hd in heads]
    qs = [q_ref[0][:, sl] for sl in cols]

    z_c = [jnp.dot(qs[hd], ckt_ref[0, 0, hd].astype(BF16), preferred_element_type=F32) for hd in heads]
    z_n = [jnp.where(causal, lax.dot_general(qs[hd], kn_ref[0][:, cols[hd]], NT,
                                             preferred_element_type=F32), MASKED) for hd in heads]
    c_n = [jnp.dot(_softplus2(z).astype(BF16), tri[:ts, :ts], preferred_element_type=F32) for z in z_n]
    sps = [_softplus2(z).astype(BF16) for z in z_c]
    locs = [jnp.dot(jnp.concatenate([sp[:, b * KEY_BLOCK:(b + 1) * KEY_BLOCK] for b in range(nblk)],
                                    axis=0), tri, preferred_element_type=F32) for sp in sps]
    outs = []
    for hd in heads:
        carry = c_n[hd][:, :1]
        cs = [None] * nblk
        for b in reversed(range(nblk)):
            loc = locs[hd][b * ts:(b + 1) * ts, :]
            cs[b] = loc + carry
            carry = carry + loc[:, :1]
        w_c = jnp.exp2(z_c[hd] - jnp.concatenate(cs, axis=1)).astype(BF16)
        w_n = jnp.exp2(z_n[hd] - c_n[hd]).astype(BF16)
        outs.append(jnp.dot(w_n, vn_ref[0][:, cols[hd]], preferred_element_type=F32)
                    + lax.dot_general(w_c, cvt_ref[0, 0, hd].astype(BF16), NT,
                                      preferred_element_type=F32))
    for p in range(SAMPLE_HEADS // 2):
        sl = slice(p * LANES, (p + 1) * LANES)
        o = jnp.concatenate(outs[2 * p:2 * p + 2], axis=-1)
        o_ref[0, :, sl] = _head_norm(o, g_ref[:, sl], ones_ref[...]).astype(BF16)


def _sb_sample(q, kn, vn, cache_kt, cache_vt, tri, ones, g):
    bsz, ts, _ = q.shape
    past = cache_kt.shape[4]
    width = SAMPLE_HEADS * HEAD_DIM
    new = pl.BlockSpec((1, ts, width), lambda b, p: (b, 0, p))
    cache = pl.BlockSpec((1, 1, SAMPLE_HEADS, HEAD_DIM, past), lambda b, p: (0, b, p, 0, 0))
    return pl.pallas_call(
        _sb_sample_kernel,
        grid=(bsz, N_HEADS // SAMPLE_HEADS),
        in_specs=[new, new, new, cache, cache,
                  pl.BlockSpec(tri.shape, lambda b, p: (0, 0)),
                  pl.BlockSpec(ones.shape, lambda b, p: (0, 0)),
                  pl.BlockSpec((1, width), lambda b, p: (0, p))],
        out_specs=new,
        out_shape=jax.ShapeDtypeStruct((bsz, ts, D_A), BF16),
        compiler_params=pltpu.CompilerParams(
            dimension_semantics=("arbitrary", "arbitrary"), vmem_limit_bytes=VMEM_LIMIT),
        name="sb_sample",
    )(q, kn, vn, cache_kt, cache_vt, tri, ones, g)


def _out_ffn_kernel(x_ref, o_ref, c_ref, wo_ref, g2_ref, wg_ref, wu_ref, wd_ref, gf_ref, y_ref,
                    acc_ref, hf_ref, ga_ref, gb_ref):
    nb, tm, d = x_ref.shape
    m = nb * tm
    n_chunks = wg_ref.shape[1] // FF_CHUNK
    assert n_chunks % 4 == 3 and n_chunks * FF_CHUNK == wg_ref.shape[1]
    mix = jnp.dot(o_ref[...].reshape(m, D_A), wo_ref[0], preferred_element_type=F32)
    mix = mix + jnp.dot(c_ref[...].reshape(m, D_C), wo_ref[1], preferred_element_type=F32)
    x1 = x_ref[...].reshape(m, d) + mix
    hf_ref[...] = (x1 * _rms_scale(x1) * g2_ref[...]).astype(BF16)
    acc_ref[...] = x1

    big = 2 * FF_CHUNK
    n_big = wg_ref.shape[1] // big
    rest = pl.ds(n_big * big, FF_CHUNK)

    def chunk(c):
        return pl.ds(pl.multiple_of(c * big, big), big)

    def gate_up(cols, gu_ref, width):
        hf = hf_ref[...]
        gu_ref[0, :, :width] = jnp.dot(hf, wg_ref[:, cols], preferred_element_type=F32)
        gu_ref[1, :, :width] = jnp.dot(hf, wu_ref[:, cols], preferred_element_type=F32)

    def down(rows, gu_ref, width):
        gate = gu_ref[0, :, :width]
        act = (_swish(gate) * gu_ref[1, :, :width]).astype(BF16)
        acc_ref[...] += jnp.dot(act, wd_ref[rows, :], preferred_element_type=F32)

    gate_up(chunk(0), ga_ref, big)

    def pair(j, _):
        c = 2 * j + 1
        down(chunk(c - 1), ga_ref, big)
        gate_up(chunk(c), gb_ref, big)
        down(chunk(c), gb_ref, big)
        gate_up(chunk(c + 1), ga_ref, big)
        return 0

    lax.fori_loop(0, (n_big - 1) // 2, pair, 0)
    gate_up(rest, gb_ref, FF_CHUNK)
    down(chunk(n_big - 1), ga_ref, big)
    down(rest, gb_ref, FF_CHUNK)
    x2 = acc_ref[...]
    y_ref[...] = (x2 * _rms_scale(x2) * gf_ref[...]).reshape(nb, tm, d)


def _out_ffn(x, o, c, wo2, g2, wg, wu, wd, gf, nb, tm):
    bsz, t, d = x.shape
    row = lambda b, i: (b, i, 0)
    act = lambda width: pl.BlockSpec((nb, tm, width), row)
    vec = lambda a: pl.BlockSpec(a.shape, lambda b, i: (0, 0))
    resident = lambda a: pl.BlockSpec(a.shape, lambda b, i: (0,) * a.ndim, pipeline_mode=pl.Buffered(1))
    return pl.pallas_call(
        _out_ffn_kernel,
        grid=(bsz // nb, t // tm),
        in_specs=[act(d), act(D_A), act(D_C),
                  resident(wo2), vec(g2), resident(wg), resident(wu), resident(wd), vec(gf)],
        out_specs=act(d),
        out_shape=jax.ShapeDtypeStruct((bsz, t, d), F32),
        scratch_shapes=[pltpu.VMEM((nb * tm, d), F32), pltpu.VMEM((nb * tm, d), BF16)]
        + [pltpu.VMEM((2, nb * tm, 2 * FF_CHUNK), F32)] * 2,
        compiler_params=pltpu.CompilerParams(
            dimension_semantics=("arbitrary", "arbitrary"), vmem_limit_bytes=VMEM_LIMIT),
        name="out_ffn",
    )(x, o, c, wo2, g2, wg, wu, wd, gf)


def _row(a):
    return a.reshape(1, -1)


def kernel(x_prompt, x_sample, cache_k, cache_v, state_conv, w_in, sb_norm_g, conv_w, conv_b,
           conv_ln_g, conv_ln_b, w_out, norm1_g, norm2_g, w_gate, w_up, w_down, final_g):
    assert w_in.shape[0] == 1, "single-layer step"
    d = x_prompt.shape[-1]

    w = w_in[0].astype(BF16)
    wkvt = w[:, D_A:3 * D_A].T.reshape(2, D_A, d)
    wo2 = w_out[0].astype(BF16).reshape(2, D_A, d)
    wg, wu, wd = w_gate[0].astype(BF16), w_up[0].astype(BF16), w_down[0].astype(BF16)
    g1, g2, gf = _row(norm1_g[0]), _row(norm2_g[0]), _row(final_g)
    sbg = _row(sb_norm_g[0])
    cw, cb = conv_w[0], _row(conv_b[0])
    lg, lb = _row(conv_ln_g[0]), _row(conv_ln_b[0])
    idx = jnp.arange(KEY_BLOCK)
    tri = (idx[:, None] >= idx[None, :]).astype(BF16)
    lane = jnp.arange(LANES) // HEAD_DIM
    ones = ((lane[:, None] == lane[None, :]) * (1.0 / HEAD_DIM)).astype(BF16)
    conv_params = (cw, cb, lg, lb)
    tail = (wo2, g2, wg, wu, wd, gf)

    q, ktb, vtb, ktf, vtf, u = _in_proj_prompt(x_prompt, g1, w, wkvt)
    c = _conv_group(u, *conv_params)
    o = _sb_prompt(q, ktb, vtb, tri, ones, sbg)
    y_prompt = _out_ffn(x_prompt, o, c, *tail, 1, FFN_TILE)
    k_prompt = jnp.swapaxes(ktf, 3, 4)
    v_prompt = jnp.swapaxes(vtf, 3, 4)
    conv_prompt = u[None, :, -CONV_HIST:, :]

    bs, ts, _ = x_sample.shape
    hist = jnp.pad(state_conv[0], ((0, 0), (HALO - CONV_HIST, 0), (0, 0)))
    qs, kbs, vbs, k_sample, v_sample, cs, us = _in_proj_sample(x_sample, g1, w, hist, *conv_params)
    os_ = _sb_sample(qs, kbs, vbs, jnp.swapaxes(cache_k, 3, 4), jnp.swapaxes(cache_v, 3, 4), tri, ones, sbg)
    y_sample = _out_ffn(x_sample, os_, cs, *tail, bs, ts)
    conv_sample = jnp.concatenate([state_conv[0], us], axis=1)[None, :, -CONV_HIST:, :]

    return (y_prompt, y_sample, k_prompt, v_prompt, conv_prompt, k_sample, v_sample, conv_sample)
```
